```python
import math
import jax, jax.numpy as jnp
from jax import lax
import numpy as np

D_MODEL = 1024
BATCH = 8
SEQ = 2048
DEPTH = 4

CHUNK = 64
Q_BLOCK = 128
N_EVEN = (DEPTH + 1) // 2
N_ODD = DEPTH // 2

POOL_WINDOWS = (2, 4, 8, 16)
POOL_GROUPS = len(POOL_WINDOWS)
POOL_GROUP_WIDTH = D_MODEL // 8
POOL_WIDTH = POOL_GROUPS * POOL_GROUP_WIDTH
LRU_WIDTH = D_MODEL
LRU_HEADS = 8
LRU_HEAD_DIM = LRU_WIDTH // LRU_HEADS
CONV_WIDTH = 4
LRU_C = 8.0
EVEN_IN_WIDTH = POOL_WIDTH + 2 * LRU_WIDTH
EVEN_MIX_WIDTH = POOL_WIDTH + LRU_WIDTH
MLA_HEADS = 8
QK_NOPE_DIM = 128
QK_ROPE_DIM = 64
V_HEAD_DIM = 128
Q_LORA_RANK = 384
KV_LORA_RANK = 256
ODD_IN_WIDTH = Q_LORA_RANK + KV_LORA_RANK + QK_ROPE_DIM
ROPE_THETA = 10000.0
D_FF = 4 * D_MODEL
DEEPNORM_ALPHA = (2 * DEPTH) ** 0.25
DEEPNORM_BETA = (8 * DEPTH) ** -0.25
LN_EPS = 1e-5
RMS_EPS = 1e-6

kernel_name = "pool_rglru_mla_deepnorm_hybrid"


def layer_norm(x, g, b):
    xf = x.astype(jnp.float32)
    mu = jnp.mean(xf, axis=-1, keepdims=True)
    var = jnp.mean(jnp.square(xf - mu), axis=-1, keepdims=True)
    y = (xf - mu) * lax.rsqrt(var + LN_EPS) * g.astype(jnp.float32) + b.astype(jnp.float32)
    return y.astype(x.dtype)


def rms_norm(x, g):
    xf = x.astype(jnp.float32)
    y = xf * lax.rsqrt(jnp.mean(jnp.square(xf), axis=-1, keepdims=True) + RMS_EPS) * g.astype(jnp.float32)
    return y.astype(x.dtype)


def rope_tables(positions):
    inv_freq = ROPE_THETA ** (-jnp.arange(0, QK_ROPE_DIM, 2, dtype=jnp.float32) / QK_ROPE_DIM)
    ang = positions.astype(jnp.float32)[..., None] * inv_freq
    return jnp.cos(ang), jnp.sin(ang)


def apply_rope(x, cos, sin):
    xf = x.astype(jnp.float32)
    x1, x2 = jnp.split(xf, 2, axis=-1)
    return jnp.concatenate([x1 * cos - x2 * sin, x1 * sin + x2 * cos], axis=-1).astype(x.dtype)


def multiscale_pool(u, w_pool, pool_scale):
    b, s, _ = u.shape
    ug = u.astype(jnp.float32).reshape(b, s, POOL_GROUPS, POOL_GROUP_WIDTH)
    csum = jnp.cumsum(ug, axis=1)
    t = jnp.arange(s)
    diffs = []
    for g, w in enumerate(POOL_WINDOWS):
        cg = csum[:, :, g]
        lagged = jnp.pad(cg, ((0, 0), (w, 0), (0, 0)))[:, :s]
        count = jnp.minimum(t + 1, w).astype(jnp.float32)[None, :, None]
        diffs.append((cg - lagged) / count - ug[:, :, g])
    d = jnp.stack(diffs, axis=2)
    y = jnp.einsum('bsgc,gcd->bsgd', d, w_pool.astype(jnp.float32))
    y = y.reshape(b, s, POOL_WIDTH) * pool_scale.astype(jnp.float32)
    return y.astype(u.dtype)


def causal_depthwise_conv(u, w, bias):
    c = u.shape[-1]
    y = lax.conv_general_dilated(u, w[:, None, :].astype(u.dtype), window_strides=(1,),
                                 padding=[(CONV_WIDTH - 1, 0)],
                                 dimension_numbers=('NWC', 'WIO', 'NWC'),
                                 feature_group_count=c)
    return y + bias.astype(u.dtype)


def rg_lru(u, w_a, b_a, w_x, b_x, lam):
    b, s, _ = u.shape
    uf = u.astype(jnp.float32)
    uh = uf.reshape(b, s, LRU_HEADS, LRU_HEAD_DIM)
    r = jax.nn.sigmoid(jnp.einsum('bshc,hcd->bshd', uh, w_a.astype(jnp.float32)).reshape(b, s, LRU_WIDTH)
                       + b_a.astype(jnp.float32))
    i = jax.nn.sigmoid(jnp.einsum('bshc,hcd->bshd', uh, w_x.astype(jnp.float32)).reshape(b, s, LRU_WIDTH)
                       + b_x.astype(jnp.float32))
    log_a = -LRU_C * r * jax.nn.softplus(-lam.astype(jnp.float32))
    a = jnp.exp(log_a)
    mult = jnp.sqrt(-jnp.expm1(2.0 * log_a))
    xin = mult * (i * uf)

    def combine(lhs, rhs):
        a1, b1 = lhs
        a2, b2 = rhs
        return a1 * a2, a2 * b1 + b2

    _, h = lax.associative_scan(combine, (a, xin), axis=1)
    return h


def pool_lru_mixer(x, w_in, w_pool, pool_scale, conv_w, conv_b, w_a, b_a, w_x, b_x, lam, w_out):
    proj = x @ w_in
    u_pool = proj[..., :POOL_WIDTH]
    u_lru = proj[..., POOL_WIDTH:POOL_WIDTH + LRU_WIDTH]
    u_gate = proj[..., POOL_WIDTH + LRU_WIDTH:]
    y_pool = multiscale_pool(u_pool, w_pool, pool_scale)
    h = rg_lru(causal_depthwise_conv(u_lru, conv_w, conv_b), w_a, b_a, w_x, b_x, lam)
    y_lru = (h * jax.nn.gelu(u_gate.astype(jnp.float32))).astype(x.dtype)
    return jnp.concatenate([y_pool, y_lru], axis=-1) @ w_out


def mla_mixer(x, cos, sin, w_down, q_norm_g, kv_norm_g, w_qb, w_kvb, w_o):
    b, s, _ = x.shape
    down = x @ w_down
    cq = rms_norm(down[..., :Q_LORA_RANK], q_norm_g)
    ckv = rms_norm(down[..., Q_LORA_RANK:Q_LORA_RANK + KV_LORA_RANK], kv_norm_g)
    k_pe = apply_rope(down[..., Q_LORA_RANK + KV_LORA_RANK:], cos, sin)
    q = (cq @ w_qb).reshape(b, s, MLA_HEADS, QK_NOPE_DIM + QK_ROPE_DIM)
    q_nope = q[..., :QK_NOPE_DIM]
    q_pe = apply_rope(q[..., QK_NOPE_DIM:], cos[:, :, None, :], sin[:, :, None, :])
    kv = (ckv @ w_kvb).reshape(b, s, MLA_HEADS, QK_NOPE_DIM + V_HEAD_DIM)
    k_nope = kv[..., :QK_NOPE_DIM]
    v = kv[..., QK_NOPE_DIM:]
    scale = (QK_NOPE_DIM + QK_ROPE_DIM) ** -0.5
    chunk_id = jnp.arange(s) // CHUNK
    neg = jnp.finfo(jnp.float32).min
    outs = []
    for qs in range(0, s, Q_BLOCK):
        ke = qs + Q_BLOCK
        sc = (jnp.einsum('bqhd,bkhd->bhqk', q_nope[:, qs:ke], k_nope[:, :ke],
                         preferred_element_type=jnp.float32)
              + jnp.einsum('bqhr,bkr->bhqk', q_pe[:, qs:ke], k_pe[:, :ke],
                           preferred_element_type=jnp.float32)) * scale
        mask = chunk_id[:ke][None, :] <= chunk_id[qs:ke][:, None]
        p = jax.nn.softmax(jnp.where(mask, sc, neg), axis=-1).astype(v.dtype)
        outs.append(jnp.einsum('bhqk,bkhv->bqhv', p, v[:, :ke]))
    o = jnp.concatenate(outs, axis=1).reshape(b, s, MLA_HEADS * V_HEAD_DIM)
    return o @ w_o


def squared_relu_mlp(x, w1, w2):
    return jnp.square(jax.nn.relu(x @ w1)) @ w2


def setup_inputs(seed: int = 0) -> dict:
    key = jax.random.key(seed)
    ks = jax.random.split(key, 28)
    f32 = jnp.float32
    nrm = lambda k, shp, sc: jax.random.normal(k, shp, f32) * sc
    x = jax.random.normal(ks[0], (BATCH, SEQ, D_MODEL), f32)
    offset = jax.random.randint(ks[1], (BATCH, 1), 0, 4096, dtype=jnp.int32)
    positions = (offset + jnp.arange(SEQ, dtype=jnp.int32)[None, :]).astype(jnp.int32)
    ln_mix_g = 1.0 + nrm(ks[2], (DEPTH, D_MODEL), 0.1)
    ln_mix_b = nrm(ks[3], (DEPTH, D_MODEL), 0.02)
    ln_ffn_g = 1.0 + nrm(ks[4], (DEPTH, D_MODEL), 0.1)
    ln_ffn_b = nrm(ks[5], (DEPTH, D_MODEL), 0.02)
    even_w_in = nrm(ks[6], (N_EVEN, D_MODEL, EVEN_IN_WIDTH), D_MODEL ** -0.5)
    pool_w = nrm(ks[7], (N_EVEN, POOL_GROUPS, POOL_GROUP_WIDTH, POOL_GROUP_WIDTH), POOL_GROUP_WIDTH ** -0.5)
    pool_scale = 1.0 + nrm(ks[8], (N_EVEN, POOL_WIDTH), 0.1)
    lru_conv_w = nrm(ks[9], (N_EVEN, CONV_WIDTH, LRU_WIDTH), CONV_WIDTH ** -0.5)
    lru_conv_b = nrm(ks[10], (N_EVEN, LRU_WIDTH), 0.02)
    lru_w_a = nrm(ks[11], (N_EVEN, LRU_HEADS, LRU_HEAD_DIM, LRU_HEAD_DIM), LRU_HEAD_DIM ** -0.5)
    lru_b_a = nrm(ks[12], (N_EVEN, LRU_WIDTH), 0.02)
    lru_w_x = nrm(ks[13], (N_EVEN, LRU_HEADS, LRU_HEAD_DIM, LRU_HEAD_DIM), LRU_HEAD_DIM ** -0.5)
    lru_b_x = nrm(ks[14], (N_EVEN, LRU_WIDTH), 0.02)
    a_c = jax.random.uniform(ks[15], (N_EVEN, LRU_WIDTH), f32, 0.9, 0.999)
    s_a = a_c ** (1.0 / LRU_C)
    lru_lambda = jnp.log(s_a) - jnp.log1p(-s_a)
    even_w_out = nrm(ks[16], (N_EVEN, EVEN_MIX_WIDTH, D_MODEL), EVEN_MIX_WIDTH ** -0.5 * DEEPNORM_BETA)
    mla_w_down = nrm(ks[17], (N_ODD, D_MODEL, ODD_IN_WIDTH), D_MODEL ** -0.5)
    mla_q_norm_g = 1.0 + nrm(ks[18], (N_ODD, Q_LORA_RANK), 0.1)
    mla_kv_norm_g = 1.0 + nrm(ks[19], (N_ODD, KV_LORA_RANK), 0.1)
    mla_w_qb = nrm(ks[20], (N_ODD, Q_LORA_RANK, MLA_HEADS * (QK_NOPE_DIM + QK_ROPE_DIM)), Q_LORA_RANK ** -0.5)
    mla_w_kvb = nrm(ks[21], (N_ODD, KV_LORA_RANK, MLA_HEADS * (QK_NOPE_DIM + V_HEAD_DIM)), KV_LORA_RANK ** -0.5)
    mla_w_o = nrm(ks[22], (N_ODD, MLA_HEADS * V_HEAD_DIM, D_MODEL), (MLA_HEADS * V_HEAD_DIM) ** -0.5 * DEEPNORM_BETA)
    mlp_w1 = nrm(ks[23], (DEPTH, D_MODEL, D_FF), D_MODEL ** -0.5)
    mlp_w2 = nrm(ks[24], (DEPTH, D_FF, D_MODEL), D_FF ** -0.5 * DEEPNORM_BETA)
    return {"x": x, "positions": positions,
            "ln_mix_g": ln_mix_g, "ln_mix_b": ln_mix_b, "ln_ffn_g": ln_ffn_g, "ln_ffn_b": ln_ffn_b,
            "even_w_in": even_w_in, "pool_w": pool_w, "pool_scale": pool_scale,
            "lru_conv_w": lru_conv_w, "lru_conv_b": lru_conv_b,
            "lru_w_a": lru_w_a, "lru_b_a": lru_b_a, "lru_w_x": lru_w_x, "lru_b_x": lru_b_x,
            "lru_lambda": lru_lambda, "even_w_out": even_w_out,
            "mla_w_down": mla_w_down, "mla_q_norm_g": mla_q_norm_g, "mla_kv_norm_g": mla_kv_norm_g,
            "mla_w_qb": mla_w_qb, "mla_w_kvb": mla_w_kvb, "mla_w_o": mla_w_o,
            "mlp_w1": mlp_w1, "mlp_w2": mlp_w2}


def reference(x, positions, ln_mix_g, ln_mix_b, ln_ffn_g, ln_ffn_b,
              even_w_in, pool_w, pool_scale, lru_conv_w, lru_conv_b,
              lru_w_a, lru_b_a, lru_w_x, lru_b_x, lru_lambda, even_w_out,
              mla_w_down, mla_q_norm_g, mla_kv_norm_g, mla_w_qb, mla_w_kvb, mla_w_o,
              mlp_w1, mlp_w2):
    cos, sin = rope_tables(positions)
    for layer in range(DEPTH):
        j = layer // 2
        if layer % 2 == 0:
            mix = pool_lru_mixer(x, even_w_in[j], pool_w[j], pool_scale[j],
                                 lru_conv_w[j], lru_conv_b[j], lru_w_a[j], lru_b_a[j],
                                 lru_w_x[j], lru_b_x[j], lru_lambda[j], even_w_out[j])
        else:
            mix = mla_mixer(x, cos, sin, mla_w_down[j], mla_q_norm_g[j], mla_kv_norm_g[j],
                            mla_w_qb[j], mla_w_kvb[j], mla_w_o[j])
        x = layer_norm(DEEPNORM_ALPHA * x + mix, ln_mix_g[layer], ln_mix_b[layer])
        x = layer_norm(DEEPNORM_ALPHA * x + squared_relu_mlp(x, mlp_w1[layer], mlp_w2[layer]),
                       ln_ffn_g[layer], ln_ffn_b[layer])
    return x
```

```python
import functools

import numpy as np
import jax
import jax.numpy as jnp
from jax import lax
from jax.experimental import pallas as pl
from jax.experimental.pallas import tpu as pltpu

D_MODEL = 1024
BATCH = 8
SEQ = 2048
DEPTH = 4
N_TOK = BATCH * SEQ

CHUNK = 64
Q_BLOCK = 128

POOL_WINDOWS = (2, 4, 8, 16)
POOL_GROUP_WIDTH = 128
POOL_WIDTH = 512
POOL_HALO = 16
LRU_WIDTH = 1024
LRU_HEADS = 8
LRU_HEAD_DIM = 128
CONV_WIDTH = 4
CONV_HALO = 8
LRU_C = 8.0
EVEN_IN_WIDTH = POOL_WIDTH + 2 * LRU_WIDTH
EVEN_MIX_WIDTH = POOL_WIDTH + LRU_WIDTH

MLA_HEADS = 8
QK_NOPE_DIM = 128
QK_ROPE_DIM = 64
ROPE_HALF = QK_ROPE_DIM // 2
V_HEAD_DIM = 128
Q_LORA_RANK = 384
KV_LORA_RANK = 256
ROPE_THETA = 10000.0
LANES = 128
SUBLANES = 8
Q_WIDTH = MLA_HEADS * QK_NOPE_DIM + (MLA_HEADS // 2) * LANES
K_WIDTH = MLA_HEADS * QK_NOPE_DIM + 2 * LANES
V_WIDTH = MLA_HEADS * V_HEAD_DIM
DOWN_WIDTH = Q_LORA_RANK + KV_LORA_RANK + LANES

D_FF = 4 * D_MODEL
FF_CHUNK = 1024
DEEPNORM_ALPHA = (2 * DEPTH) ** 0.25
LN_EPS = 1e-5
RMS_EPS = 1e-6

MIX_TILE = 256
TOK_TILE = 512
VMEM_LIMIT = 56 * 1024 * 1024

F32 = jnp.float32
BF16 = jnp.bfloat16


def _resident(shape):
    zeros = (0,) * len(shape)
    return pl.BlockSpec(shape, lambda *_: zeros, pipeline_mode=pl.Buffered(1))


def _layer_norm(v, g, b):
    mu = jnp.mean(v, axis=-1, keepdims=True)
    d = v - mu
    var = jnp.mean(d * d, axis=-1, keepdims=True)
    return d * lax.rsqrt(var + LN_EPS) * g + b


def _rms_norm(v, g):
    return v * lax.rsqrt(jnp.mean(v * v, axis=-1, keepdims=True) + RMS_EPS) * g


def _sigmoid(z):
    return 0.5 * jnp.tanh(0.5 * z) + 0.5


def _even_mix_kernel(x_ref, w_in_ref, pool_w_ref, pool_scale_ref, conv_w_ref, conv_b_ref,
                     wa_ref, wx_ref, ba_ref, bx_ref, lam_ref, y_ref,
                     pool_ext, lru_ext, carry, h_buf):
    t = pl.program_id(1)
    T = MIX_TILE

    @pl.when(t == 0)
    def _():
        pool_ext[0:POOL_HALO, :] = jnp.zeros((POOL_HALO, POOL_WIDTH), F32)
        lru_ext[0:CONV_HALO, :] = jnp.zeros((CONV_HALO, LRU_WIDTH), F32)
        carry[...] = jnp.zeros((SUBLANES, LRU_WIDTH), F32)

    xb = x_ref[...].astype(BF16)
    proj = jnp.dot(xb, w_in_ref[...], preferred_element_type=F32)
    u_pool = proj[:, :POOL_WIDTH]
    u_lru = proj[:, POOL_WIDTH:POOL_WIDTH + LRU_WIDTH]
    u_gate = proj[:, POOL_WIDTH + LRU_WIDTH:]
    pool_ext[POOL_HALO:, :] = u_pool
    lru_ext[CONV_HALO:, :] = u_lru

    frame = lax.broadcasted_iota(jnp.int32, (T, 1), 0) + t * T
    for g, w in enumerate(POOL_WINDOWS):
        lo, hi = g * POOL_GROUP_WIDTH, (g + 1) * POOL_GROUP_WIDTH
        ug = u_pool[:, lo:hi]
        acc = ug
        for k in range(1, w):
            acc = acc + pool_ext[POOL_HALO - k:POOL_HALO - k + T, lo:hi]
        count = jnp.minimum(frame + 1, w).astype(F32)
        d = acc / count - ug
        yg = jnp.dot(d.astype(BF16), pool_w_ref[g], preferred_element_type=F32)
        y_ref[:, lo:hi] = (yg * pool_scale_ref[:, lo:hi]).astype(BF16)
    pool_ext[0:POOL_HALO, :] = u_pool[T - POOL_HALO:, :]

    cw = conv_w_ref[...]
    c = cw[CONV_WIDTH - 1:CONV_WIDTH, :] * u_lru + conv_b_ref[...]
    for k in range(CONV_WIDTH - 1):
        back = CONV_WIDTH - 1 - k
        c = c + cw[k:k + 1, :] * lru_ext[CONV_HALO - back:CONV_HALO - back + T, :]
    lru_ext[0:CONV_HALO, :] = u_lru[T - CONV_HALO:, :]

    cb = c.astype(BF16)
    za, zx = [], []
    for p in range(LRU_HEADS // 2):
        cp = cb[:, p * 2 * LRU_HEAD_DIM:(p + 1) * 2 * LRU_HEAD_DIM]
        za.append(jnp.dot(cp, wa_ref[p], preferred_element_type=F32))
        zx.append(jnp.dot(cp, wx_ref[p], preferred_element_type=F32))
    r = _sigmoid(jnp.concatenate(za, axis=-1) + ba_ref[...])
    i = _sigmoid(jnp.concatenate(zx, axis=-1) + bx_ref[...])

    z = -lam_ref[...]
    softplus = jnp.maximum(z, 0.0) + jnp.log1p(jnp.exp(-jnp.abs(z)))
    log_a = r * (-LRU_C * softplus)
    a = jnp.exp(log_a)
    mult = jnp.sqrt(jnp.tanh(-log_a) * (a * a + 1.0))
    xin = mult * (i * c)

    A = a.reshape(T // SUBLANES, SUBLANES, LRU_WIDTH)
    Bv = xin.reshape(T // SUBLANES, SUBLANES, LRU_WIDTH)
    sub = lax.broadcasted_iota(jnp.int32, (1, SUBLANES, LRU_WIDTH), 1)
    for k in (1, 2, 4):
        seen = sub >= k
        a_prev = jnp.where(seen, pltpu.roll(A, k, axis=1), 1.0)
        b_prev = jnp.where(seen, pltpu.roll(Bv, k, axis=1), 0.0)
        Bv = A * b_prev + Bv
        A = A * a_prev
    h_prev = carry[...]
    for j in range(T // SUBLANES):
        hj = A[j] * h_prev + Bv[j]
        h_buf[j * SUBLANES:(j + 1) * SUBLANES, :] = hj
        h_prev = jnp.broadcast_to(hj[SUBLANES - 1:SUBLANES, :], (SUBLANES, LRU_WIDTH))
    carry[...] = h_prev

    y_lru = h_buf[...] * jax.nn.gelu(u_gate)
    y_ref[:, POOL_WIDTH:] = y_lru.astype(BF16)


def _even_mix(x, w_in, pool_w, pool_scale, conv_w, conv_b, wa, wx, ba, bx, lam):
    T = MIX_TILE
    return pl.pallas_call(
        _even_mix_kernel,
        grid=(BATCH, SEQ // T),
        in_specs=[
            pl.BlockSpec((None, T, D_MODEL), lambda b, t: (b, t, 0)),
            _resident((D_MODEL, EVEN_IN_WIDTH)),
            _resident((len(POOL_WINDOWS), POOL_GROUP_WIDTH, POOL_GROUP_WIDTH)),
            _resident((1, POOL_WIDTH)),
            _resident((CONV_WIDTH, LRU_WIDTH)),
            _resident((1, LRU_WIDTH)),
            _resident((LRU_HEADS // 2, 2 * LRU_HEAD_DIM, 2 * LRU_HEAD_DIM)),
            _resident((LRU_HEADS // 2, 2 * LRU_HEAD_DIM, 2 * LRU_HEAD_DIM)),
            _resident((1, LRU_WIDTH)),
            _resident((1, LRU_WIDTH)),
            _resident((1, LRU_WIDTH)),
        ],
        out_specs=pl.BlockSpec((None, T, EVEN_MIX_WIDTH), lambda b, t: (b, t, 0)),
        out_shape=jax.ShapeDtypeStruct((BATCH, SEQ, EVEN_MIX_WIDTH), BF16),
        scratch_shapes=[
            pltpu.VMEM((T + POOL_HALO, POOL_WIDTH), F32),
            pltpu.VMEM((T + CONV_HALO, LRU_WIDTH), F32),
            pltpu.VMEM((SUBLANES, LRU_WIDTH), F32),
            pltpu.VMEM((T, LRU_WIDTH), F32),
        ],
        compiler_params=pltpu.CompilerParams(
            dimension_semantics=("arbitrary", "arbitrary"),
            vmem_limit_bytes=VMEM_LIMIT),
        name="even_mix",
    )(x, w_in, pool_w, pool_scale, conv_w, conv_b, wa, wx, ba, bx, lam)


def _out_mlp_kernel(y_ref, x_ref, w_out_ref, g1_ref, b1_ref, w1_ref, w2_ref, g2_ref, b2_ref,
                    o_ref):
    mix = jnp.dot(y_ref[...], w_out_ref[...], preferred_element_type=F32)
    x1 = _layer_norm(DEEPNORM_ALPHA * x_ref[...] + mix, g1_ref[...], b1_ref[...])
    xb = x1.astype(BF16)
    acc = None
    for c in range(D_FF // FF_CHUNK):
        lo, hi = c * FF_CHUNK, (c + 1) * FF_CHUNK
        h = jnp.dot(xb, w1_ref[:, lo:hi], preferred_element_type=F32)
        h = jnp.square(jnp.maximum(h, 0.0)).astype(BF16)
        part = jnp.dot(h, w2_ref[lo:hi, :], preferred_element_type=F32)
        acc = part if acc is None else acc + part
    o_ref[...] = _layer_norm(DEEPNORM_ALPHA * x1 + acc, g2_ref[...], b2_ref[...])


def _out_mlp(y, x, w_out, g1, b1, w1, w2, g2, b2):
    T = TOK_TILE
    kd = y.shape[-1]
    return pl.pallas_call(
        _out_mlp_kernel,
        grid=(N_TOK // T,),
        in_specs=[
            pl.BlockSpec((T, kd), lambda i: (i, 0)),
            pl.BlockSpec((T, D_MODEL), lambda i: (i, 0)),
            _resident((kd, D_MODEL)),
            _resident((1, D_MODEL)),
            _resident((1, D_MODEL)),
            _resident((D_MODEL, D_FF)),
            _resident((D_FF, D_MODEL)),
            _resident((1, D_MODEL)),
            _resident((1, D_MODEL)),
        ],
        out_specs=pl.BlockSpec((T, D_MODEL), lambda i: (i, 0)),
        out_shape=jax.ShapeDtypeStruct((N_TOK, D_MODEL), F32),
        compiler_params=pltpu.CompilerParams(
            dimension_semantics=("arbitrary",),
            vmem_limit_bytes=VMEM_LIMIT),
        name="out_mlp",
    )(y, x, w_out, g1, b1, w1, w2, g2, b2)


def _mla_proj_kernel(x_ref, pos_ref, invf_ref, w_down_ref, qg_ref, kvg_ref, w_qb_ref, w_kvb_ref,
                     q_ref, k_ref, v_ref):
    xb = x_ref[...].astype(BF16)
    down = jnp.dot(xb, w_down_ref[...], preferred_element_type=F32)
    cq = _rms_norm(down[:, :Q_LORA_RANK], qg_ref[...])
    ckv = _rms_norm(down[:, Q_LORA_RANK:Q_LORA_RANK + KV_LORA_RANK], kvg_ref[...])
    kpe = down[:, Q_LORA_RANK + KV_LORA_RANK:]

    ang = pos_ref[...].astype(F32) * invf_ref[...]
    cos = jnp.cos(ang)
    lane = lax.broadcasted_iota(jnp.int32, (1, LANES), 1)
    sin = jnp.sin(ang) * jnp.where(lane < 2 * ROPE_HALF, -1.0, 1.0)

    def rope(blk):
        return blk * cos + pltpu.roll(blk, 2 * ROPE_HALF, axis=1) * sin

    q = jnp.dot(cq.astype(BF16), w_qb_ref[...], preferred_element_type=F32)
    n_nope = MLA_HEADS * QK_NOPE_DIM
    q_ref[:, :n_nope] = q[:, :n_nope].astype(BF16)
    for p in range(MLA_HEADS // 2):
        lo = n_nope + p * LANES
        q_ref[:, lo:lo + LANES] = rope(q[:, lo:lo + LANES]).astype(BF16)

    kv = jnp.dot(ckv.astype(BF16), w_kvb_ref[...], preferred_element_type=F32)
    k_ref[:, :n_nope] = kv[:, :n_nope].astype(BF16)
    kr = rope(kpe)
    first_slot = (lane % (2 * ROPE_HALF)) < ROPE_HALF
    k_ref[:, n_nope:n_nope + LANES] = jnp.where(first_slot, kr, 0.0).astype(BF16)
    k_ref[:, n_nope + LANES:] = jnp.where(first_slot, 0.0, kr).astype(BF16)
    v_ref[...] = kv[:, n_nope:].astype(BF16)


def _mla_proj(x, pos, invf, w_down, qg, kvg, w_qb, w_kvb):
    T = TOK_TILE
    return pl.pallas_call(
        _mla_proj_kernel,
        grid=(N_TOK // T,),
        in_specs=[
            pl.BlockSpec((T, D_MODEL), lambda i: (i, 0)),
            pl.BlockSpec((T, 1), lambda i: (i, 0)),
            _resident((1, LANES)),
            _resident((D_MODEL, DOWN_WIDTH)),
            _resident((1, Q_LORA_RANK)),
            _resident((1, KV_LORA_RANK)),
            _resident((Q_LORA_RANK, Q_WIDTH)),
            _resident((KV_LORA_RANK, 2 * V_WIDTH)),
        ],
        out_specs=[
            pl.BlockSpec((T, Q_WIDTH), lambda i: (i, 0)),
            pl.BlockSpec((T, K_WIDTH), lambda i: (i, 0)),
            pl.BlockSpec((T, V_WIDTH), lambda i: (i, 0)),
        ],
        out_shape=[
            jax.ShapeDtypeStruct((N_TOK, Q_WIDTH), BF16),
            jax.ShapeDtypeStruct((N_TOK, K_WIDTH), BF16),
            jax.ShapeDtypeStruct((N_TOK, V_WIDTH), BF16),
        ],
        compiler_params=pltpu.CompilerParams(
            dimension_semantics=("arbitrary",),
            vmem_limit_bytes=VMEM_LIMIT),
        name="mla_proj",
    )(x, pos, invf, w_down, qg, kvg, w_qb, w_kvb)


_NT = (((1,), (1,)), ((), ()))


def _mla_attn_kernel(qn_ref, qp_ref, kn_ref, kp_ref, v_ref, o_ref):
    scale = (QK_NOPE_DIM + QK_ROPE_DIM) ** -0.5
    neg = jnp.finfo(F32).min
    row = lax.broadcasted_iota(jnp.int32, (Q_BLOCK, Q_BLOCK), 0)
    col = lax.broadcasted_iota(jnp.int32, (Q_BLOCK, Q_BLOCK), 1)
    hidden = (col // CHUNK) > (row // CHUNK)
    for j in range(SEQ // Q_BLOCK):
        qs, ke = j * Q_BLOCK, (j + 1) * Q_BLOCK
        s = (lax.dot_general(qn_ref[qs:ke, :], kn_ref[:ke, :], _NT, preferred_element_type=F32)
             + lax.dot_general(qp_ref[qs:ke, :], kp_ref[:ke, :], _NT, preferred_element_type=F32))
        s = s * scale
        diag = jnp.where(hidden, neg, s[:, qs:ke])
        s = diag if j == 0 else jnp.concatenate([s[:, :qs], diag], axis=-1)
        m = jnp.max(s, axis=-1, keepdims=True)
        p = jnp.exp(s - m)
        l = jnp.sum(p, axis=-1, keepdims=True)
        o = jnp.dot(p.astype(BF16), v_ref[:ke, :], preferred_element_type=F32)
        o_ref[qs:ke, :] = (o / l).astype(BF16)


def _mla_attn(q, k, v):
    n_nope_blocks = MLA_HEADS
    blk = lambda f: pl.BlockSpec((None, SEQ, LANES), f)
    return pl.pallas_call(
        _mla_attn_kernel,
        grid=(BATCH, MLA_HEADS),
        in_specs=[
            blk(lambda b, h: (b, 0, h)),
            blk(lambda b, h: (b, 0, n_nope_blocks + h // 2)),
            blk(lambda b, h: (b, 0, h)),
            blk(lambda b, h: (b, 0, n_nope_blocks + h % 2)),
            blk(lambda b, h: (b, 0, h)),
        ],
        out_specs=blk(lambda b, h: (b, 0, h)),
        out_shape=jax.ShapeDtypeStruct((BATCH, SEQ, V_WIDTH), BF16),
        compiler_params=pltpu.CompilerParams(
            dimension_semantics=("arbitrary", "arbitrary"),
            vmem_limit_bytes=VMEM_LIMIT),
        name="mla_attn",
    )(q, q, k, k, v)


def _q_columns():
    head = QK_NOPE_DIM + QK_ROPE_DIM
    nope = [h * head + np.arange(QK_NOPE_DIM) for h in range(MLA_HEADS)]
    pe = []
    for p in range(MLA_HEADS // 2):
        for half in range(2):
            for h in (2 * p, 2 * p + 1):
                pe.append(h * head + QK_NOPE_DIM + half * ROPE_HALF + np.arange(ROPE_HALF))
    return np.concatenate(nope + pe)


def _kv_columns():
    head = QK_NOPE_DIM + V_HEAD_DIM
    k = [h * head + np.arange(QK_NOPE_DIM) for h in range(MLA_HEADS)]
    v = [h * head + QK_NOPE_DIM + np.arange(V_HEAD_DIM) for h in range(MLA_HEADS)]
    return np.concatenate(k + v)


def _down_columns():
    base = Q_LORA_RANK + KV_LORA_RANK
    x1 = base + np.arange(ROPE_HALF)
    x2 = base + ROPE_HALF + np.arange(ROPE_HALF)
    return np.concatenate([np.arange(base), x1, x1, x2, x2])


def _pair_block_diag(w):
    w = w.reshape(LRU_HEADS // 2, 2, LRU_HEAD_DIM, LRU_HEAD_DIM)
    z = jnp.zeros_like(w[:, 0])
    top = jnp.concatenate([w[:, 0], z], axis=-1)
    bot = jnp.concatenate([z, w[:, 1]], axis=-1)
    return jnp.concatenate([top, bot], axis=-2)


def kernel(x, positions, ln_mix_g, ln_mix_b, ln_ffn_g, ln_ffn_b, even_w_in, pool_w, pool_scale, lru_conv_w, lru_conv_b, lru_w_a, lru_b_a, lru_w_x, lru_b_x, lru_lambda, even_w_out, mla_w_down, mla_q_norm_g, mla_kv_norm_g, mla_w_qb, mla_w_kvb, mla_w_o, mlp_w1, mlp_w2):
    row = lambda v: v.reshape(1, -1)
    inv_freq = ROPE_THETA ** (-jnp.arange(0, QK_ROPE_DIM, 2, dtype=F32) / QK_ROPE_DIM)
    invf = jnp.tile(inv_freq, LANES // ROPE_HALF).reshape(1, LANES)
    pos = positions.reshape(N_TOK, 1)
    q_cols, kv_cols, down_cols = _q_columns(), _kv_columns(), _down_columns()

    xf = x.reshape(N_TOK, D_MODEL)
    for layer in range(DEPTH):
        j = layer // 2
        if layer % 2 == 0:
            y = _even_mix(
                xf.reshape(BATCH, SEQ, D_MODEL), even_w_in[j].astype(BF16), pool_w[j].astype(BF16),
                row(pool_scale[j]), lru_conv_w[j], row(lru_conv_b[j]),
                _pair_block_diag(lru_w_a[j]).astype(BF16), _pair_block_diag(lru_w_x[j]).astype(BF16),
                row(lru_b_a[j]), row(lru_b_x[j]), row(lru_lambda[j]))
            y = y.reshape(N_TOK, EVEN_MIX_WIDTH)
            w_out = even_w_out[j]
        else:
            q, k, v = _mla_proj(
                xf, pos, invf, mla_w_down[j][:, down_cols].astype(BF16),
                row(mla_q_norm_g[j]), row(mla_kv_norm_g[j]),
                mla_w_qb[j][:, q_cols].astype(BF16), mla_w_kvb[j][:, kv_cols].astype(BF16))
            y = _mla_attn(q.reshape(BATCH, SEQ, Q_WIDTH), k.reshape(BATCH, SEQ, K_WIDTH),
                          v.reshape(BATCH, SEQ, V_WIDTH)).reshape(N_TOK, V_WIDTH)
            w_out = mla_w_o[j]
        xf = _out_mlp(y, xf, w_out.astype(BF16), row(ln_mix_g[layer]), row(ln_mix_b[layer]),
                      mlp_w1[layer].astype(BF16), mlp_w2[layer].astype(BF16),
                      row(ln_ffn_g[layer]), row(ln_ffn_b[layer]))
    return xf.reshape(BATCH, SEQ, D_MODEL)
```

```python
import functools

import numpy as np
import jax
import jax.numpy as jnp
from jax import lax
from jax.experimental import pallas as pl
from jax.experimental.pallas import tpu as pltpu

D_MODEL = 1024
BATCH = 8
SEQ = 2048
DEPTH = 4
N_TOK = BATCH * SEQ

CHUNK = 64
Q_BLOCK = 128

POOL_WINDOWS = (2, 4, 8, 16)
POOL_GROUP_WIDTH = 128
POOL_WIDTH = 512
POOL_HALO = 16
LRU_WIDTH = 1024
LRU_HEADS = 8
LRU_HEAD_DIM = 128
CONV_WIDTH = 4
CONV_HALO = 8
LRU_C = 8.0
EVEN_IN_WIDTH = POOL_WIDTH + 2 * LRU_WIDTH
EVEN_MIX_WIDTH = POOL_WIDTH + LRU_WIDTH

MLA_HEADS = 8
QK_NOPE_DIM = 128
QK_ROPE_DIM = 64
ROPE_HALF = QK_ROPE_DIM // 2
V_HEAD_DIM = 128
Q_LORA_RANK = 384
KV_LORA_RANK = 256
ROPE_THETA = 10000.0
LANES = 128
SUBLANES = 8
QB_WIDTH = MLA_HEADS * QK_NOPE_DIM + (MLA_HEADS // 2) * LANES
QK_HEAD_WIDTH = 2 * LANES
QK_WIDTH = MLA_HEADS * QK_HEAD_WIDTH
V_WIDTH = MLA_HEADS * V_HEAD_DIM
ATTN_Q_ROWS = 512
ATTN_HEADS_PER_STEP = 4
DOWN_WIDTH = Q_LORA_RANK + KV_LORA_RANK + LANES

D_FF = 4 * D_MODEL
FF_CHUNK = 1024
DEEPNORM_ALPHA = (2 * DEPTH) ** 0.25
LN_EPS = 1e-5
RMS_EPS = 1e-6

MIX_TILE = 256
TOK_TILE = 512
VMEM_LIMIT = 56 * 1024 * 1024

F32 = jnp.float32
BF16 = jnp.bfloat16
F32_TINY = float(np.finfo(np.float32).tiny)


def _resident(shape):
    zeros = (0,) * len(shape)
    return pl.BlockSpec(shape, lambda *_: zeros, pipeline_mode=pl.Buffered(1))


def _layer_norm(v, g, b):
    mu = jnp.mean(v, axis=-1, keepdims=True)
    d = v - mu
    var = jnp.mean(d * d, axis=-1, keepdims=True)
    return d * lax.rsqrt(var + LN_EPS) * g + b


def _rms_norm(v, g):
    return v * lax.rsqrt(jnp.mean(v * v, axis=-1, keepdims=True) + RMS_EPS) * g


def _even_mix_kernel(x_ref, w_in_ref, pool_w_ref, pool_scale_ref, conv_w_ref, conv_b_ref,
                     wa_ref, wx_ref, ba_ref, bx_ref, lam_ref, y_ref,
                     pool_ext, lru_ext, carry, h_buf):
    t = pl.program_id(1)
    T = MIX_TILE

    @pl.when(t == 0)
    def _():
        pool_ext[0:POOL_HALO, :] = jnp.zeros((POOL_HALO, POOL_WIDTH), F32)
        lru_ext[0:CONV_HALO, :] = jnp.zeros((CONV_HALO, LRU_WIDTH), F32)
        carry[...] = jnp.zeros((SUBLANES, LRU_WIDTH), F32)

    xb = x_ref[...].astype(BF16)
    proj = jnp.dot(xb, w_in_ref[...], preferred_element_type=F32)
    u_pool = proj[:, :POOL_WIDTH]
    u_lru = proj[:, POOL_WIDTH:POOL_WIDTH + LRU_WIDTH]
    u_gate = proj[:, POOL_WIDTH + LRU_WIDTH:]
    pool_ext[POOL_HALO:, :] = u_pool
    lru_ext[CONV_HALO:, :] = u_lru

    frame = lax.broadcasted_iota(jnp.int32, (T, 1), 0) + t * T
    for g, w in enumerate(POOL_WINDOWS):
        lo, hi = g * POOL_GROUP_WIDTH, (g + 1) * POOL_GROUP_WIDTH
        ug = u_pool[:, lo:hi]
        acc = pool_ext[:, lo:hi]
        k = 1
        while k < w:
            acc = acc + pltpu.roll(acc, k, axis=0)
            k *= 2
        acc = acc[POOL_HALO:, :]
        count = jnp.minimum(frame + 1, w).astype(F32)
        d = acc / count - ug
        yg = jnp.dot(d.astype(BF16), pool_w_ref[g], preferred_element_type=F32)
        y_ref[:, lo:hi] = (yg * pool_scale_ref[:, lo:hi]).astype(BF16)
    pool_ext[0:POOL_HALO, :] = u_pool[T - POOL_HALO:, :]

    cw = conv_w_ref[...]
    c = cw[CONV_WIDTH - 1:CONV_WIDTH, :] * u_lru + conv_b_ref[...]
    for k in range(CONV_WIDTH - 1):
        back = CONV_WIDTH - 1 - k
        c = c + cw[k:k + 1, :] * lru_ext[CONV_HALO - back:CONV_HALO - back + T, :]
    lru_ext[0:CONV_HALO, :] = u_lru[T - CONV_HALO:, :]

    cb = c.astype(BF16)
    za, zx = [], []
    for p in range(LRU_HEADS // 2):
        cp = cb[:, p * 2 * LRU_HEAD_DIM:(p + 1) * 2 * LRU_HEAD_DIM]
        za.append(jnp.dot(cp, wa_ref[p], preferred_element_type=F32))
        zx.append(jnp.dot(cp, wx_ref[p], preferred_element_type=F32))
    ta = jnp.tanh(jnp.concatenate(za, axis=-1) + ba_ref[...])
    tx = jnp.tanh(jnp.concatenate(zx, axis=-1) + bx_ref[...])

    z = -lam_ref[...]
    softplus = jnp.maximum(z, 0.0) + jnp.log1p(jnp.exp(-jnp.abs(z)))
    half_rate = (-0.5 * LRU_C) * softplus
    log_a = ta * half_rate + half_rate
    a = jnp.exp(log_a)
    m2 = jnp.tanh(-log_a) * (a * a + 1.0)
    mult = m2 * lax.rsqrt(jnp.maximum(m2, F32_TINY))
    half_c = 0.5 * c
    xin = mult * (tx * half_c + half_c)

    A = a.reshape(T // SUBLANES, SUBLANES, LRU_WIDTH)
    Bv = xin.reshape(T // SUBLANES, SUBLANES, LRU_WIDTH)
    sub = lax.broadcasted_iota(jnp.int32, (1, SUBLANES, LRU_WIDTH), 1)
    for k in (1, 2, 4):
        seen = sub >= k
        a_prev = jnp.where(seen, pltpu.roll(A, k, axis=1), 1.0)
        b_prev = jnp.where(seen, pltpu.roll(Bv, k, axis=1), 0.0)
        Bv = A * b_prev + Bv
        A = A * a_prev
    h_prev = carry[...]
    for j in range(T // SUBLANES):
        hj = A[j] * h_prev + Bv[j]
        h_buf[j * SUBLANES:(j + 1) * SUBLANES, :] = hj
        h_prev = jnp.broadcast_to(hj[SUBLANES - 1:SUBLANES, :], (SUBLANES, LRU_WIDTH))
    carry[...] = h_prev

    y_lru = h_buf[...] * jax.nn.gelu(u_gate)
    y_ref[:, POOL_WIDTH:] = y_lru.astype(BF16)


def _even_mix(x, w_in, pool_w, pool_scale, conv_w, conv_b, wa, wx, ba, bx, lam):
    T = MIX_TILE
    return pl.pallas_call(
        _even_mix_kernel,
        grid=(BATCH, SEQ // T),
        in_specs=[
            pl.BlockSpec((None, T, D_MODEL), lambda b, t: (b, t, 0)),
            _resident((D_MODEL, EVEN_IN_WIDTH)),
            _resident((len(POOL_WINDOWS), POOL_GROUP_WIDTH, POOL_GROUP_WIDTH)),
            _resident((1, POOL_WIDTH)),
            _resident((CONV_WIDTH, LRU_WIDTH)),
            _resident((1, LRU_WIDTH)),
            _resident((LRU_HEADS // 2, 2 * LRU_HEAD_DIM, 2 * LRU_HEAD_DIM)),
            _resident((LRU_HEADS // 2, 2 * LRU_HEAD_DIM, 2 * LRU_HEAD_DIM)),
            _resident((1, LRU_WIDTH)),
            _resident((1, LRU_WIDTH)),
            _resident((1, LRU_WIDTH)),
        ],
        out_specs=pl.BlockSpec((None, T, EVEN_MIX_WIDTH), lambda b, t: (b, t, 0)),
        out_shape=jax.ShapeDtypeStruct((BATCH, SEQ, EVEN_MIX_WIDTH), BF16),
        scratch_shapes=[
            pltpu.VMEM((T + POOL_HALO, POOL_WIDTH), F32),
            pltpu.VMEM((T + CONV_HALO, LRU_WIDTH), F32),
            pltpu.VMEM((SUBLANES, LRU_WIDTH), F32),
            pltpu.VMEM((T, LRU_WIDTH), F32),
        ],
        compiler_params=pltpu.CompilerParams(
            dimension_semantics=("arbitrary", "arbitrary"),
            vmem_limit_bytes=VMEM_LIMIT),
        name="even_mix",
    )(x, w_in, pool_w, pool_scale, conv_w, conv_b, wa, wx, ba, bx, lam)


def _out_mlp_kernel(y_ref, x_ref, w_out_ref, g1_ref, b1_ref, w1_ref, w2_ref, g2_ref, b2_ref,
                    o_ref):
    mix = jnp.dot(y_ref[...], w_out_ref[...], preferred_element_type=F32)
    x1 = _layer_norm(DEEPNORM_ALPHA * x_ref[...] + mix, g1_ref[...], b1_ref[...])
    xb = x1.astype(BF16)
    acc = None
    for c in range(D_FF // FF_CHUNK):
        lo, hi = c * FF_CHUNK, (c + 1) * FF_CHUNK
        h = jnp.dot(xb, w1_ref[:, lo:hi], preferred_element_type=F32)
        h = jnp.square(jnp.maximum(h, 0.0)).astype(BF16)
        part = jnp.dot(h, w2_ref[lo:hi, :], preferred_element_type=F32)
        acc = part if acc is None else acc + part
    o_ref[...] = _layer_norm(DEEPNORM_ALPHA * x1 + acc, g2_ref[...], b2_ref[...])


def _out_mlp(y, x, w_out, g1, b1, w1, w2, g2, b2):
    T = TOK_TILE
    kd = y.shape[-1]
    return pl.pallas_call(
        _out_mlp_kernel,
        grid=(N_TOK // T,),
        in_specs=[
            pl.BlockSpec((T, kd), lambda i: (i, 0)),
            pl.BlockSpec((T, D_MODEL), lambda i: (i, 0)),
            _resident((kd, D_MODEL)),
            _resident((1, D_MODEL)),
            _resident((1, D_MODEL)),
            _resident((D_MODEL, D_FF)),
            _resident((D_FF, D_MODEL)),
            _resident((1, D_MODEL)),
            _resident((1, D_MODEL)),
        ],
        out_specs=pl.BlockSpec((T, D_MODEL), lambda i: (i, 0)),
        out_shape=jax.ShapeDtypeStruct((N_TOK, D_MODEL), F32),
        compiler_params=pltpu.CompilerParams(
            dimension_semantics=("arbitrary",),
            vmem_limit_bytes=VMEM_LIMIT),
        name="out_mlp",
    )(y, x, w_out, g1, b1, w1, w2, g2, b2)


def _mla_proj_kernel(x_ref, pos_ref, invf_ref, w_down_ref, qg_ref, kvg_ref, w_qb_ref, w_kvb_ref,
                     q_ref, k_ref, v_ref):
    xb = x_ref[...].astype(BF16)
    down = jnp.dot(xb, w_down_ref[...], preferred_element_type=F32)
    cq = _rms_norm(down[:, :Q_LORA_RANK], qg_ref[...])
    ckv = _rms_norm(down[:, Q_LORA_RANK:Q_LORA_RANK + KV_LORA_RANK], kvg_ref[...])
    kpe = down[:, Q_LORA_RANK + KV_LORA_RANK:]

    ang = pos_ref[...].astype(F32) * invf_ref[...]
    cos = jnp.cos(ang)
    lane = lax.broadcasted_iota(jnp.int32, (1, LANES), 1)
    sin = jnp.sin(ang) * jnp.where(lane < 2 * ROPE_HALF, -1.0, 1.0)

    def rope(blk):
        return blk * cos + pltpu.roll(blk, 2 * ROPE_HALF, axis=1) * sin

    q = jnp.dot(cq.astype(BF16), w_qb_ref[...], preferred_element_type=F32)
    n_nope = MLA_HEADS * QK_NOPE_DIM
    q_pe = [rope(q[:, n_nope + p * LANES:n_nope + (p + 1) * LANES]).astype(BF16)
            for p in range(MLA_HEADS // 2)]
    kv = jnp.dot(ckv.astype(BF16), w_kvb_ref[...], preferred_element_type=F32)
    kr = rope(kpe)
    first_slot = (lane % (2 * ROPE_HALF)) < ROPE_HALF
    k_pe = [jnp.where(first_slot, kr, 0.0).astype(BF16),
            jnp.where(first_slot, 0.0, kr).astype(BF16)]
    for h in range(MLA_HEADS):
        lo = h * QK_HEAD_WIDTH
        q_ref[:, lo:lo + LANES] = q[:, h * LANES:(h + 1) * LANES].astype(BF16)
        q_ref[:, lo + LANES:lo + 2 * LANES] = q_pe[h // 2]
        k_ref[:, lo:lo + LANES] = kv[:, h * LANES:(h + 1) * LANES].astype(BF16)
        k_ref[:, lo + LANES:lo + 2 * LANES] = k_pe[h % 2]
    v_ref[...] = kv[:, n_nope:].astype(BF16)


def _mla_proj(x, pos, invf, w_down, qg, kvg, w_qb, w_kvb):
    T = TOK_TILE
    return pl.pallas_call(
        _mla_proj_kernel,
        grid=(N_TOK // T,),
        in_specs=[
            pl.BlockSpec((T, D_MODEL), lambda i: (i, 0)),
            pl.BlockSpec((T, 1), lambda i: (i, 0)),
            _resident((1, LANES)),
            _resident((D_MODEL, DOWN_WIDTH)),
            _resident((1, Q_LORA_RANK)),
            _resident((1, KV_LORA_RANK)),
            _resident((Q_LORA_RANK, QB_WIDTH)),
            _resident((KV_LORA_RANK, 2 * V_WIDTH)),
        ],
        out_specs=[
            pl.BlockSpec((T, QK_WIDTH), lambda i: (i, 0)),
            pl.BlockSpec((T, QK_WIDTH), lambda i: (i, 0)),
            pl.BlockSpec((T, V_WIDTH), lambda i: (i, 0)),
        ],
        out_shape=[
            jax.ShapeDtypeStruct((N_TOK, QK_WIDTH), BF16),
            jax.ShapeDtypeStruct((N_TOK, QK_WIDTH), BF16),
            jax.ShapeDtypeStruct((N_TOK, V_WIDTH), BF16),
        ],
        compiler_params=pltpu.CompilerParams(
            dimension_semantics=("arbitrary",),
            vmem_limit_bytes=VMEM_LIMIT),
        name="mla_proj",
    )(x, pos, invf, w_down, qg, kvg, w_qb, w_kvb)


_NT = (((1,), (1,)), ((), ()))


def _mla_attn_kernel(q_ref, k_ref, v_ref, o_ref, v_ones):
    c = (QK_NOPE_DIM + QK_ROPE_DIM) ** -0.5 * float(np.log2(np.e))
    neg = jnp.finfo(F32).min
    R = ATTN_Q_ROWS
    row = lax.broadcasted_iota(jnp.int32, (R, R), 0)
    col = lax.broadcasted_iota(jnp.int32, (R, R), 1)
    hidden = (col // CHUNK) > (row // CHUNK)
    for hh in range(ATTN_HEADS_PER_STEP):
        v_ones[hh, :, :V_HEAD_DIM] = v_ref[:, hh * V_HEAD_DIM:(hh + 1) * V_HEAD_DIM]
        v_ones[hh, :, V_HEAD_DIM:] = jnp.ones((SEQ, LANES), BF16)
    for j in range(SEQ // R):
        qs, ke = j * R, (j + 1) * R
        for hh in range(ATTN_HEADS_PER_STEP):
            lo, hi = hh * QK_HEAD_WIDTH, (hh + 1) * QK_HEAD_WIDTH
            s = lax.dot_general(q_ref[qs:ke, lo:hi], k_ref[:ke, lo:hi], _NT,
                                preferred_element_type=F32) * c
            diag = jnp.where(hidden, neg, s[:, qs:ke])
            s = diag if j == 0 else jnp.concatenate([s[:, :qs], diag], axis=-1)
            m = jnp.max(s, axis=-1, keepdims=True)
            p = jnp.exp2(s - m).astype(BF16)
            ol = jnp.dot(p, v_ones[hh, :ke, :], preferred_element_type=F32)
            o = ol[:, :V_HEAD_DIM] / ol[:, V_HEAD_DIM:V_HEAD_DIM + 1]
            o_ref[qs:ke, hh * V_HEAD_DIM:(hh + 1) * V_HEAD_DIM] = o.astype(BF16)


def _mla_attn(q, k, v):
    hps = ATTN_HEADS_PER_STEP
    return pl.pallas_call(
        _mla_attn_kernel,
        grid=(BATCH, MLA_HEADS // hps),
        in_specs=[
            pl.BlockSpec((None, SEQ, hps * QK_HEAD_WIDTH), lambda b, h: (b, 0, h)),
            pl.BlockSpec((None, SEQ, hps * QK_HEAD_WIDTH), lambda b, h: (b, 0, h)),
            pl.BlockSpec((None, SEQ, hps * V_HEAD_DIM), lambda b, h: (b, 0, h)),
        ],
        out_specs=pl.BlockSpec((None, SEQ, hps * V_HEAD_DIM), lambda b, h: (b, 0, h)),
        out_shape=jax.ShapeDtypeStruct((BATCH, SEQ, V_WIDTH), BF16),
        scratch_shapes=[pltpu.VMEM((hps, SEQ, V_HEAD_DIM + LANES), BF16)],
        compiler_params=pltpu.CompilerParams(
            dimension_semantics=("arbitrary", "arbitrary"),
            vmem_limit_bytes=VMEM_LIMIT),
        name="mla_attn",
    )(q, k, v)


def _q_columns():
    head = QK_NOPE_DIM + QK_ROPE_DIM
    nope = [h * head + np.arange(QK_NOPE_DIM) for h in range(MLA_HEADS)]
    pe = []
    for p in range(MLA_HEADS // 2):
        for half in range(2):
            for h in (2 * p, 2 * p + 1):
                pe.append(h * head + QK_NOPE_DIM + half * ROPE_HALF + np.arange(ROPE_HALF))
    return np.concatenate(nope + pe)


def _kv_columns():
    head = QK_NOPE_DIM + V_HEAD_DIM
    k = [h * head + np.arange(QK_NOPE_DIM) for h in range(MLA_HEADS)]
    v = [h * head + QK_NOPE_DIM + np.arange(V_HEAD_DIM) for h in range(MLA_HEADS)]
    return np.concatenate(k + v)


def _down_columns():
    base = Q_LORA_RANK + KV_LORA_RANK
    x1 = base + np.arange(ROPE_HALF)
    x2 = base + ROPE_HALF + np.arange(ROPE_HALF)
    return np.concatenate([np.arange(base), x1, x1, x2, x2])


def _pair_block_diag(w):
    w = w.reshape(LRU_HEADS // 2, 2, LRU_HEAD_DIM, LRU_HEAD_DIM)
    z = jnp.zeros_like(w[:, 0])
    top = jnp.concatenate([w[:, 0], z], axis=-1)
    bot = jnp.concatenate([z, w[:, 1]], axis=-1)
    return jnp.concatenate([top, bot], axis=-2)


def kernel(x, positions, ln_mix_g, ln_mix_b, ln_ffn_g, ln_ffn_b, even_w_in, pool_w, pool_scale, lru_conv_w, lru_conv_b, lru_w_a, lru_b_a, lru_w_x, lru_b_x, lru_lambda, even_w_out, mla_w_down, mla_q_norm_g, mla_kv_norm_g, mla_w_qb, mla_w_kvb, mla_w_o, mlp_w1, mlp_w2):
    row = lambda v: v.reshape(1, -1)
    inv_freq = ROPE_THETA ** (-jnp.arange(0, QK_ROPE_DIM, 2, dtype=F32) / QK_ROPE_DIM)
    invf = jnp.tile(inv_freq, LANES // ROPE_HALF).reshape(1, LANES)
    pos = positions.reshape(N_TOK, 1)
    q_cols, kv_cols, down_cols = _q_columns(), _kv_columns(), _down_columns()

    xf = x.reshape(N_TOK, D_MODEL)
    for layer in range(DEPTH):
        j = layer // 2
        if layer % 2 == 0:
            y = _even_mix(
                xf.reshape(BATCH, SEQ, D_MODEL), even_w_in[j].astype(BF16), pool_w[j].astype(BF16),
                row(pool_scale[j]), lru_conv_w[j], row(lru_conv_b[j]),
                _pair_block_diag(0.5 * lru_w_a[j]).astype(BF16),
                _pair_block_diag(0.5 * lru_w_x[j]).astype(BF16),
                row(0.5 * lru_b_a[j]), row(0.5 * lru_b_x[j]), row(lru_lambda[j]))
            y = y.reshape(N_TOK, EVEN_MIX_WIDTH)
            w_out = even_w_out[j]
        else:
            q, k, v = _mla_proj(
                xf, pos, invf, mla_w_down[j][:, down_cols].astype(BF16),
                row(mla_q_norm_g[j]), row(mla_kv_norm_g[j]),
                mla_w_qb[j][:, q_cols].astype(BF16), mla_w_kvb[j][:, kv_cols].astype(BF16))
            y = _mla_attn(q.reshape(BATCH, SEQ, QK_WIDTH), k.reshape(BATCH, SEQ, QK_WIDTH),
                          v.reshape(BATCH, SEQ, V_WIDTH)).reshape(N_TOK, V_WIDTH)
            w_out = mla_w_o[j]
        xf = _out_mlp(y, xf, w_out.astype(BF16), row(ln_mix_g[layer]), row(ln_mix_b[layer]),
                      mlp_w1[layer].astype(BF16), mlp_w2[layer].astype(BF16),
                      row(ln_ffn_g[layer]), row(ln_ffn_b[layer]))
    return xf.reshape(BATCH, SEQ, D_MODEL)
```

```python
import numpy as np
import jax
import jax.numpy as jnp
from jax import lax
from jax.experimental import pallas as pl
from jax.experimental.pallas import tpu as pltpu

D_MODEL = 1024
BATCH = 8
SEQ = 2048
DEPTH = 4
N_TOK = BATCH * SEQ

CHUNK = 64

LANES = 128
SUBLANES = 8
assert BATCH == SUBLANES

POOL_WINDOWS = (2, 4, 8, 16)
POOL_GROUP_WIDTH = 128
POOL_WIDTH = 512
POOL_HIST = 16
LRU_WIDTH = 1024
LRU_HEADS = 8
LRU_HEAD_DIM = 128
CONV_WIDTH = 4
CONV_HIST = CONV_WIDTH - 1
LRU_C = 8.0
EVEN_IN_WIDTH = POOL_WIDTH + 2 * LRU_WIDTH
EVEN_MIX_WIDTH = POOL_WIDTH + LRU_WIDTH

MLA_HEADS = 8
QK_NOPE_DIM = 128
QK_ROPE_DIM = 64
ROPE_HALF = QK_ROPE_DIM // 2
V_HEAD_DIM = 128
Q_LORA_RANK = 384
KV_LORA_RANK = 256
ROPE_THETA = 10000.0
DOWN_WIDTH = Q_LORA_RANK + KV_LORA_RANK + LANES
QB_WIDTH = MLA_HEADS * QK_NOPE_DIM + (MLA_HEADS // 2) * LANES
QK_HEAD_WIDTH = 2 * LANES
QK_WIDTH = MLA_HEADS * QK_HEAD_WIDTH
V_WIDTH = MLA_HEADS * V_HEAD_DIM
ATTN_Q_ROWS = 512
ATTN_HEADS_PER_STEP = 4

D_FF = 4 * D_MODEL
FF_CHUNK = 1024
DEEPNORM_ALPHA = (2 * DEPTH) ** 0.25
LN_EPS = 1e-5
RMS_EPS = 1e-6

MIX_FRAMES = 64
MIX_ROWS = MIX_FRAMES * BATCH
MIX_SUBTILES = 2
TOK_TILE = 512
VMEM_LIMIT = 56 * 1024 * 1024

F32 = jnp.float32
BF16 = jnp.bfloat16
F32_TINY = float(np.finfo(np.float32).tiny)


def _resident(shape):
    zeros = (0,) * len(shape)
    return pl.BlockSpec(shape, lambda *_: zeros, pipeline_mode=pl.Buffered(1))


def _layer_norm(v, g, b):
    mu = jnp.mean(v, axis=-1, keepdims=True)
    d = v - mu
    var = jnp.mean(d * d, axis=-1, keepdims=True)
    return d * lax.rsqrt(var + LN_EPS) * g + b


def _rms_norm(v, g):
    return v * lax.rsqrt(jnp.mean(v * v, axis=-1, keepdims=True) + RMS_EPS) * g


def _even_mix_kernel(x_ref, w_in_ref, pool_w_ref, pool_scale_ref, conv_w_ref, conv_b_ref,
                     wa_ref, wx_ref, ba_ref, bx_ref, lam_ref, y_ref,
                     pool_ext, lru_ext, carry, h_buf):
    t = pl.program_id(0)
    PH = POOL_HIST * BATCH
    CH = CONV_HIST * BATCH

    @pl.when(t == 0)
    def _():
        pool_ext[0:PH, :] = jnp.zeros((PH, POOL_WIDTH), F32)
        lru_ext[0:CH, :] = jnp.zeros((CH, LRU_WIDTH), F32)
        carry[...] = jnp.zeros((BATCH, LRU_WIDTH), F32)

    h = carry[...]
    for s in range(MIX_SUBTILES):
        h = _even_mix_rows(s, t, h, x_ref, w_in_ref, pool_w_ref, pool_scale_ref, conv_w_ref,
                           conv_b_ref, wa_ref, wx_ref, ba_ref, bx_ref, lam_ref, y_ref,
                           pool_ext, lru_ext, h_buf)
    carry[...] = h
    pool_ext[0:PH, :] = pool_ext[MIX_ROWS:MIX_ROWS + PH, :]
    lru_ext[0:CH, :] = lru_ext[MIX_ROWS:MIX_ROWS + CH, :]


def _even_mix_rows(s, t, h, x_ref, w_in_ref, pool_w_ref, pool_scale_ref, conv_w_ref, conv_b_ref,
                   wa_ref, wx_ref, ba_ref, bx_ref, lam_ref, y_ref, pool_ext, lru_ext, h_buf):
    R = MIX_ROWS // MIX_SUBTILES
    r0 = s * R
    PH = POOL_HIST * BATCH
    CH = CONV_HIST * BATCH

    xb = x_ref[r0:r0 + R, :].astype(BF16)
    proj = jnp.dot(xb, w_in_ref[...], preferred_element_type=F32)
    u_pool = proj[:, :POOL_WIDTH]
    u_lru = proj[:, POOL_WIDTH:POOL_WIDTH + LRU_WIDTH]
    u_gate = proj[:, POOL_WIDTH + LRU_WIDTH:]
    pool_ext[PH + r0:PH + r0 + R, :] = u_pool
    lru_ext[CH + r0:CH + r0 + R, :] = u_lru

    frame = (lax.broadcasted_iota(jnp.int32, (R, 1), 0) + r0) // BATCH + t * MIX_FRAMES
    for g, w in enumerate(POOL_WINDOWS):
        lo, hi = g * POOL_GROUP_WIDTH, (g + 1) * POOL_GROUP_WIDTH
        acc = pool_ext[r0:r0 + PH + R, lo:hi]
        k = 1
        while k < w:
            acc = acc[k * BATCH:, :] + acc[:acc.shape[0] - k * BATCH, :]
            k *= 2
        acc = acc[acc.shape[0] - R:, :]
        count = jnp.minimum(frame + 1, w).astype(F32)
        d = acc / count - u_pool[:, lo:hi]
        yg = jnp.dot(d.astype(BF16), pool_w_ref[g], preferred_element_type=F32)
        y_ref[r0:r0 + R, lo:hi] = (yg * pool_scale_ref[:, lo:hi]).astype(BF16)

    cw = conv_w_ref[...]
    c = cw[CONV_WIDTH - 1:CONV_WIDTH, :] * u_lru + conv_b_ref[...]
    for k in range(CONV_WIDTH - 1):
        start = r0 + CH - (CONV_WIDTH - 1 - k) * BATCH
        c = c + cw[k:k + 1, :] * lru_ext[start:start + R, :]

    cb = c.astype(BF16)
    za, zx = [], []
    for p in range(LRU_HEADS // 2):
        cp = cb[:, p * 2 * LRU_HEAD_DIM:(p + 1) * 2 * LRU_HEAD_DIM]
        za.append(jnp.dot(cp, wa_ref[p], preferred_element_type=F32))
        zx.append(jnp.dot(cp, wx_ref[p], preferred_element_type=F32))
    ta = jnp.tanh(jnp.concatenate(za, axis=-1) + ba_ref[...])
    tx = jnp.tanh(jnp.concatenate(zx, axis=-1) + bx_ref[...])

    z = -lam_ref[...]
    softplus = jnp.maximum(z, 0.0) + jnp.log1p(jnp.exp(-jnp.abs(z)))
    half_rate = (-0.5 * LRU_C) * softplus
    log_a = ta * half_rate + half_rate
    a = jnp.exp(log_a)
    m2 = jnp.tanh(-log_a) * (a * a + 1.0)
    mult = m2 * lax.rsqrt(jnp.maximum(m2, F32_TINY))
    half_c = 0.5 * c
    xin = mult * (tx * half_c + half_c)

    for j in range(R // BATCH):
        rows = slice(j * BATCH, (j + 1) * BATCH)
        h = a[rows, :] * h + xin[rows, :]
        h_buf[r0 + j * BATCH:r0 + (j + 1) * BATCH, :] = h

    y_lru = h_buf[r0:r0 + R, :] * jax.nn.gelu(u_gate)
    y_ref[r0:r0 + R, POOL_WIDTH:] = y_lru.astype(BF16)
    return h


def _even_mix(x, w_in, pool_w, pool_scale, conv_w, conv_b, wa, wx, ba, bx, lam):
    R = MIX_ROWS
    return pl.pallas_call(
        _even_mix_kernel,
        grid=(SEQ // MIX_FRAMES,),
        in_specs=[
            pl.BlockSpec((R, D_MODEL), lambda t: (t, 0)),
            _resident((D_MODEL, EVEN_IN_WIDTH)),
            _resident((len(POOL_WINDOWS), POOL_GROUP_WIDTH, POOL_GROUP_WIDTH)),
            _resident((1, POOL_WIDTH)),
            _resident((CONV_WIDTH, LRU_WIDTH)),
            _resident((1, LRU_WIDTH)),
            _resident((LRU_HEADS // 2, 2 * LRU_HEAD_DIM, 2 * LRU_HEAD_DIM)),
            _resident((LRU_HEADS // 2, 2 * LRU_HEAD_DIM, 2 * LRU_HEAD_DIM)),
            _resident((1, LRU_WIDTH)),
            _resident((1, LRU_WIDTH)),
            _resident((1, LRU_WIDTH)),
        ],
        out_specs=pl.BlockSpec((R, EVEN_MIX_WIDTH), lambda t: (t, 0)),
        out_shape=jax.ShapeDtypeStruct((N_TOK, EVEN_MIX_WIDTH), BF16),
        scratch_shapes=[
            pltpu.VMEM((R + POOL_HIST * BATCH, POOL_WIDTH), F32),
            pltpu.VMEM((R + CONV_HIST * BATCH, LRU_WIDTH), F32),
            pltpu.VMEM((BATCH, LRU_WIDTH), F32),
            pltpu.VMEM((R, LRU_WIDTH), F32),
        ],
        compiler_params=pltpu.CompilerParams(
            dimension_semantics=("arbitrary",),
            vmem_limit_bytes=VMEM_LIMIT),
        name="even_mix",
    )(x, w_in, pool_w, pool_scale, conv_w, conv_b, wa, wx, ba, bx, lam)


def _out_mlp_kernel(y_ref, x_ref, w_out_ref, g1_ref, b1_ref, w1_ref, w2_ref, g2_ref, b2_ref,
                    o_ref):
    mix = jnp.dot(y_ref[...], w_out_ref[...], preferred_element_type=F32)
    x1 = _layer_norm(DEEPNORM_ALPHA * x_ref[...] + mix, g1_ref[...], b1_ref[...])
    xb = x1.astype(BF16)
    acc = None
    for c in range(D_FF // FF_CHUNK):
        lo, hi = c * FF_CHUNK, (c + 1) * FF_CHUNK
        h = jnp.dot(xb, w1_ref[:, lo:hi], preferred_element_type=F32)
        h = jnp.square(jnp.maximum(h, 0.0)).astype(BF16)
        part = jnp.dot(h, w2_ref[lo:hi, :], preferred_element_type=F32)
        acc = part if acc is None else acc + part
    o_ref[...] = _layer_norm(DEEPNORM_ALPHA * x1 + acc, g2_ref[...], b2_ref[...])


def _out_mlp(y, x, w_out, g1, b1, w1, w2, g2, b2):
    T = TOK_TILE
    kd = y.shape[-1]
    return pl.pallas_call(
        _out_mlp_kernel,
        grid=(N_TOK // T,),
        in_specs=[
            pl.BlockSpec((T, kd), lambda i: (i, 0)),
            pl.BlockSpec((T, D_MODEL), lambda i: (i, 0)),
            _resident((kd, D_MODEL)),
            _resident((1, D_MODEL)),
            _resident((1, D_MODEL)),
            _resident((D_MODEL, D_FF)),
            _resident((D_FF, D_MODEL)),
            _resident((1, D_MODEL)),
            _resident((1, D_MODEL)),
        ],
        out_specs=pl.BlockSpec((T, D_MODEL), lambda i: (i, 0)),
        out_shape=jax.ShapeDtypeStruct((N_TOK, D_MODEL), F32),
        compiler_params=pltpu.CompilerParams(
            dimension_semantics=("arbitrary",),
            vmem_limit_bytes=VMEM_LIMIT),
        name="out_mlp",
    )(y, x, w_out, g1, b1, w1, w2, g2, b2)


def _mla_proj_kernel(x_ref, pos_ref, invf_ref, w_down_ref, qg_ref, kvg_ref, w_qb_ref, w_kvb_ref,
                     q_ref, k_ref, v_ref):
    xb = x_ref[...].astype(BF16)
    down = jnp.dot(xb, w_down_ref[...], preferred_element_type=F32)
    cq = _rms_norm(down[:, :Q_LORA_RANK], qg_ref[...])
    ckv = _rms_norm(down[:, Q_LORA_RANK:Q_LORA_RANK + KV_LORA_RANK], kvg_ref[...])
    kpe = down[:, Q_LORA_RANK + KV_LORA_RANK:]

    ang = pos_ref[...].astype(F32) * invf_ref[...]
    cos = jnp.cos(ang)
    lane = lax.broadcasted_iota(jnp.int32, (1, LANES), 1)
    sin = jnp.sin(ang) * jnp.where(lane < 2 * ROPE_HALF, -1.0, 1.0)

    def rope(blk):
        return blk * cos + pltpu.roll(blk, 2 * ROPE_HALF, axis=1) * sin

    q = jnp.dot(cq.astype(BF16), w_qb_ref[...], preferred_element_type=F32)
    n_nope = MLA_HEADS * QK_NOPE_DIM
    q_pe = [rope(q[:, n_nope + p * LANES:n_nope + (p + 1) * LANES]).astype(BF16)
            for p in range(MLA_HEADS // 2)]
    kv = jnp.dot(ckv.astype(BF16), w_kvb_ref[...], preferred_element_type=F32)
    kr = rope(kpe)
    first_slot = (lane % (2 * ROPE_HALF)) < ROPE_HALF
    k_pe = [jnp.where(first_slot, kr, 0.0).astype(BF16),
            jnp.where(first_slot, 0.0, kr).astype(BF16)]
    for h in range(MLA_HEADS):
        lo = h * QK_HEAD_WIDTH
        q_ref[:, lo:lo + LANES] = q[:, h * LANES:(h + 1) * LANES].astype(BF16)
        q_ref[:, lo + LANES:lo + 2 * LANES] = q_pe[h // 2]
        k_ref[:, lo:lo + LANES] = kv[:, h * LANES:(h + 1) * LANES].astype(BF16)
        k_ref[:, lo + LANES:lo + 2 * LANES] = k_pe[h % 2]
    v_ref[...] = kv[:, n_nope:].astype(BF16)


def _mla_proj(x, pos, invf, w_down, qg, kvg, w_qb, w_kvb):
    T = TOK_TILE
    return pl.pallas_call(
        _mla_proj_kernel,
        grid=(N_TOK // T,),
        in_specs=[
            pl.BlockSpec((T, D_MODEL), lambda i: (i, 0)),
            pl.BlockSpec((T, 1), lambda i: (i, 0)),
            _resident((1, LANES)),
            _resident((D_MODEL, DOWN_WIDTH)),
            _resident((1, Q_LORA_RANK)),
            _resident((1, KV_LORA_RANK)),
            _resident((Q_LORA_RANK, QB_WIDTH)),
            _resident((KV_LORA_RANK, 2 * V_WIDTH)),
        ],
        out_specs=[
            pl.BlockSpec((T, QK_WIDTH), lambda i: (i, 0)),
            pl.BlockSpec((T, QK_WIDTH), lambda i: (i, 0)),
            pl.BlockSpec((T, V_WIDTH), lambda i: (i, 0)),
        ],
        out_shape=[
            jax.ShapeDtypeStruct((N_TOK, QK_WIDTH), BF16),
            jax.ShapeDtypeStruct((N_TOK, QK_WIDTH), BF16),
            jax.ShapeDtypeStruct((N_TOK, V_WIDTH), BF16),
        ],
        compiler_params=pltpu.CompilerParams(
            dimension_semantics=("arbitrary",),
            vmem_limit_bytes=VMEM_LIMIT),
        name="mla_proj",
    )(x, pos, invf, w_down, qg, kvg, w_qb, w_kvb)


_NT = (((1,), (1,)), ((), ()))


def _mla_attn_kernel(q_ref, k_ref, v_ref, o_ref, v_ones):
    c = (QK_NOPE_DIM + QK_ROPE_DIM) ** -0.5 * float(np.log2(np.e))
    neg = jnp.finfo(F32).min
    R = ATTN_Q_ROWS
    row = lax.broadcasted_iota(jnp.int32, (R, R), 0)
    col = lax.broadcasted_iota(jnp.int32, (R, R), 1)
    hidden = (col // CHUNK) > (row // CHUNK)
    for hh in range(ATTN_HEADS_PER_STEP):
        v_ones[hh, :, :V_HEAD_DIM] = v_ref[:, hh * V_HEAD_DIM:(hh + 1) * V_HEAD_DIM]
        v_ones[hh, :, V_HEAD_DIM:] = jnp.ones((SEQ, LANES), BF16)
    for j in range(SEQ // R):
        qs, ke = j * R, (j + 1) * R
        for hh in range(ATTN_HEADS_PER_STEP):
            lo, hi = hh * QK_HEAD_WIDTH, (hh + 1) * QK_HEAD_WIDTH
            s = lax.dot_general(q_ref[qs:ke, lo:hi], k_ref[:ke, lo:hi], _NT,
                                preferred_element_type=F32) * c
            diag = jnp.where(hidden, neg, s[:, qs:ke])
            s = diag if j == 0 else jnp.concatenate([s[:, :qs], diag], axis=-1)
            m = jnp.max(s, axis=-1, keepdims=True)
            p = jnp.exp2(s - m).astype(BF16)
            ol = jnp.dot(p, v_ones[hh, :ke, :], preferred_element_type=F32)
            o = ol[:, :V_HEAD_DIM] / ol[:, V_HEAD_DIM:V_HEAD_DIM + 1]
            o_ref[qs:ke, hh * V_HEAD_DIM:(hh + 1) * V_HEAD_DIM] = o.astype(BF16)


def _mla_attn(q, k, v):
    hps = ATTN_HEADS_PER_STEP
    groups = MLA_HEADS // hps
    col = lambda b, g: (0, b * groups + g)
    return pl.pallas_call(
        _mla_attn_kernel,
        grid=(BATCH, groups),
        in_specs=[
            pl.BlockSpec((SEQ, hps * QK_HEAD_WIDTH), col),
            pl.BlockSpec((SEQ, hps * QK_HEAD_WIDTH), col),
            pl.BlockSpec((SEQ, hps * V_HEAD_DIM), col),
        ],
        out_specs=pl.BlockSpec((SEQ, hps * V_HEAD_DIM), col),
        out_shape=jax.ShapeDtypeStruct((SEQ, BATCH * V_WIDTH), BF16),
        scratch_shapes=[pltpu.VMEM((hps, SEQ, V_HEAD_DIM + LANES), BF16)],
        compiler_params=pltpu.CompilerParams(
            dimension_semantics=("arbitrary", "arbitrary"),
            vmem_limit_bytes=VMEM_LIMIT),
        name="mla_attn",
    )(q, k, v)


def _q_weight(w_qb):
    w = w_qb.reshape(Q_LORA_RANK, MLA_HEADS, QK_NOPE_DIM + QK_ROPE_DIM)
    nope = w[:, :, :QK_NOPE_DIM].reshape(Q_LORA_RANK, MLA_HEADS * QK_NOPE_DIM)
    pe = w[:, :, QK_NOPE_DIM:].reshape(Q_LORA_RANK, MLA_HEADS // 2, 2, 2, ROPE_HALF)
    pe = pe.transpose(0, 1, 3, 2, 4).reshape(Q_LORA_RANK, (MLA_HEADS // 2) * LANES)
    return jnp.concatenate([nope, pe], axis=-1).astype(BF16)


def _kv_weight(w_kvb):
    w = w_kvb.reshape(KV_LORA_RANK, MLA_HEADS, 2, QK_NOPE_DIM)
    return w.transpose(0, 2, 1, 3).reshape(KV_LORA_RANK, 2 * V_WIDTH).astype(BF16)


def _down_weight(w_down):
    base = Q_LORA_RANK + KV_LORA_RANK
    x1 = w_down[:, base:base + ROPE_HALF]
    x2 = w_down[:, base + ROPE_HALF:]
    return jnp.concatenate([w_down[:, :base], x1, x1, x2, x2], axis=-1).astype(BF16)


def _pair_block_diag(w):
    w = w.reshape(LRU_HEADS // 2, 2, LRU_HEAD_DIM, LRU_HEAD_DIM)
    z = jnp.zeros_like(w[:, 0])
    top = jnp.concatenate([w[:, 0], z], axis=-1)
    bot = jnp.concatenate([z, w[:, 1]], axis=-1)
    return jnp.concatenate([top, bot], axis=-2)


def kernel(x, positions, ln_mix_g, ln_mix_b, ln_ffn_g, ln_ffn_b, even_w_in, pool_w, pool_scale, lru_conv_w, lru_conv_b, lru_w_a, lru_b_a, lru_w_x, lru_b_x, lru_lambda, even_w_out, mla_w_down, mla_q_norm_g, mla_kv_norm_g, mla_w_qb, mla_w_kvb, mla_w_o, mlp_w1, mlp_w2):
    row = lambda v: v.reshape(1, -1)
    inv_freq = ROPE_THETA ** (-jnp.arange(0, QK_ROPE_DIM, 2, dtype=F32) / QK_ROPE_DIM)
    invf = jnp.tile(inv_freq, LANES // ROPE_HALF).reshape(1, LANES)
    pos = positions.T.reshape(N_TOK, 1)

    xf = x.transpose(1, 0, 2).reshape(N_TOK, D_MODEL)
    for layer in range(DEPTH):
        j = layer // 2
        if layer % 2 == 0:
            y = _even_mix(
                xf, even_w_in[j].astype(BF16), pool_w[j].astype(BF16),
                row(pool_scale[j]), lru_conv_w[j], row(lru_conv_b[j]),
                _pair_block_diag(0.5 * lru_w_a[j]).astype(BF16),
                _pair_block_diag(0.5 * lru_w_x[j]).astype(BF16),
                row(0.5 * lru_b_a[j]), row(0.5 * lru_b_x[j]), row(lru_lambda[j]))
            w_out = even_w_out[j]
        else:
            q, k, v = _mla_proj(
                xf, pos, invf, _down_weight(mla_w_down[j]),
                row(mla_q_norm_g[j]), row(mla_kv_norm_g[j]),
                _q_weight(mla_w_qb[j]), _kv_weight(mla_w_kvb[j]))
            y = _mla_attn(q.reshape(SEQ, BATCH * QK_WIDTH), k.reshape(SEQ, BATCH * QK_WIDTH),
                          v.reshape(SEQ, BATCH * V_WIDTH)).reshape(N_TOK, V_WIDTH)
            w_out = mla_w_o[j]
        xf = _out_mlp(y, xf, w_out.astype(BF16), row(ln_mix_g[layer]), row(ln_mix_b[layer]),
                      mlp_w1[layer].astype(BF16), mlp_w2[layer].astype(BF16),
                      row(ln_ffn_g[layer]), row(ln_ffn_b[layer]))
    return xf.reshape(SEQ, BATCH, D_MODEL).transpose(1, 0, 2)
```

```python
import numpy as np
import jax
import jax.numpy as jnp
from jax import lax
from jax.experimental import pallas as pl
from jax.experimental.pallas import tpu as pltpu

D_MODEL = 1024
BATCH = 8
SEQ = 2048
DEPTH = 4
N_TOK = BATCH * SEQ

CHUNK = 64

LANES = 128
SUBLANES = 8
assert BATCH == SUBLANES

POOL_WINDOWS = (2, 4, 8, 16)
POOL_GROUP_WIDTH = 128
POOL_WIDTH = 512
POOL_HIST = 16
LRU_WIDTH = 1024
LRU_HEADS = 8
LRU_HEAD_DIM = 128
CONV_WIDTH = 4
CONV_HIST = CONV_WIDTH - 1
LRU_C = 8.0
EVEN_IN_WIDTH = POOL_WIDTH + 2 * LRU_WIDTH
EVEN_MIX_WIDTH = POOL_WIDTH + LRU_WIDTH

MLA_HEADS = 8
QK_NOPE_DIM = 128
QK_ROPE_DIM = 64
ROPE_HALF = QK_ROPE_DIM // 2
V_HEAD_DIM = 128
Q_LORA_RANK = 384
KV_LORA_RANK = 256
ROPE_THETA = 10000.0
DOWN_WIDTH = Q_LORA_RANK + KV_LORA_RANK + LANES
QB_WIDTH = MLA_HEADS * QK_NOPE_DIM + (MLA_HEADS // 2) * LANES
QK_HEAD_WIDTH = 2 * LANES
QK_WIDTH = MLA_HEADS * QK_HEAD_WIDTH
V_WIDTH = MLA_HEADS * V_HEAD_DIM
ATTN_Q_ROWS = 512
ATTN_HEADS_PER_STEP = 4

D_FF = 4 * D_MODEL
FF_CHUNK = 1024
DEEPNORM_ALPHA = (2 * DEPTH) ** 0.25
LN_EPS = 1e-5
RMS_EPS = 1e-6

MIX_FRAMES = 64
MIX_ROWS = MIX_FRAMES * BATCH
MIX_SUBTILES = 2
TOK_TILE = 512
VMEM_LIMIT = 56 * 1024 * 1024

F32 = jnp.float32
BF16 = jnp.bfloat16
F32_TINY = float(np.finfo(np.float32).tiny)


def _resident(shape):
    zeros = (0,) * len(shape)
    return pl.BlockSpec(shape, lambda *_: zeros, pipeline_mode=pl.Buffered(1))


def _layer_norm(v, g, b):
    mu = jnp.mean(v, axis=-1, keepdims=True)
    d = v - mu
    var = jnp.mean(d * d, axis=-1, keepdims=True)
    return d * lax.rsqrt(var + LN_EPS) * g + b


def _rms_norm(v, g):
    return v * lax.rsqrt(jnp.mean(v * v, axis=-1, keepdims=True) + RMS_EPS) * g


def _even_mix_kernel(x_ref, w_in_ref, pool_w_ref, pool_scale_ref, conv_w_ref, conv_b_ref,
                     wa_ref, wx_ref, ba_ref, bx_ref, lam_ref, y_ref,
                     pool_ext, lru_ext, carry, h_buf, x_tm, y_tm):
    t = pl.program_id(0)
    PH = POOL_HIST * BATCH
    CH = CONV_HIST * BATCH

    @pl.when(t == 0)
    def _():
        pool_ext[0:PH, :] = jnp.zeros((PH, POOL_WIDTH), F32)
        lru_ext[0:CH, :] = jnp.zeros((CH, LRU_WIDTH), F32)
        carry[...] = jnp.zeros((BATCH, LRU_WIDTH), F32)

    h = carry[...]
    for s in range(MIX_SUBTILES):
        h = _even_mix_rows(s, t, h, x_ref, w_in_ref, pool_w_ref, pool_scale_ref, conv_w_ref,
                           conv_b_ref, wa_ref, wx_ref, ba_ref, bx_ref, lam_ref, y_ref,
                           pool_ext, lru_ext, h_buf, x_tm, y_tm)
    carry[...] = h
    pool_ext[0:PH, :] = pool_ext[MIX_ROWS:MIX_ROWS + PH, :]
    lru_ext[0:CH, :] = lru_ext[MIX_ROWS:MIX_ROWS + CH, :]


def _even_mix_rows(s, t, h, x_ref, w_in_ref, pool_w_ref, pool_scale_ref, conv_w_ref, conv_b_ref,
                   wa_ref, wx_ref, ba_ref, bx_ref, lam_ref, y_ref, pool_ext, lru_ext, h_buf,
                   x_tm, y_tm):
    R = MIX_ROWS // MIX_SUBTILES
    F = MIX_FRAMES // MIX_SUBTILES
    r0, f0 = s * R, s * F
    PH = POOL_HIST * BATCH
    CH = CONV_HIST * BATCH

    for b in range(BATCH):
        for cs in range(D_MODEL // LANES):
            x_tm[cs, pl.ds(r0 + b, F, stride=BATCH), :] = (
                x_ref[b, f0:f0 + F, cs * LANES:(cs + 1) * LANES])
    xb = jnp.concatenate([x_tm[cs, r0:r0 + R, :] for cs in range(D_MODEL // LANES)],
                         axis=-1).astype(BF16)
    proj = jnp.dot(xb, w_in_ref[...], preferred_element_type=F32)
    u_pool = proj[:, :POOL_WIDTH]
    u_lru = proj[:, POOL_WIDTH:POOL_WIDTH + LRU_WIDTH]
    u_gate = proj[:, POOL_WIDTH + LRU_WIDTH:]
    pool_ext[PH + r0:PH + r0 + R, :] = u_pool
    lru_ext[CH + r0:CH + r0 + R, :] = u_lru

    frame = (lax.broadcasted_iota(jnp.int32, (R, 1), 0) + r0) // BATCH + t * MIX_FRAMES
    for g, w in enumerate(POOL_WINDOWS):
        lo, hi = g * POOL_GROUP_WIDTH, (g + 1) * POOL_GROUP_WIDTH
        acc = pool_ext[r0:r0 + PH + R, lo:hi]
        k = 1
        while k < w:
            acc = acc[k * BATCH:, :] + acc[:acc.shape[0] - k * BATCH, :]
            k *= 2
        acc = acc[acc.shape[0] - R:, :]
        count = jnp.minimum(frame + 1, w).astype(F32)
        d = acc / count - u_pool[:, lo:hi]
        yg = jnp.dot(d.astype(BF16), pool_w_ref[g], preferred_element_type=F32)
        y_tm[g, r0:r0 + R, :] = yg * pool_scale_ref[:, lo:hi]

    cw = conv_w_ref[...]
    c = cw[CONV_WIDTH - 1:CONV_WIDTH, :] * u_lru + conv_b_ref[...]
    for k in range(CONV_WIDTH - 1):
        start = r0 + CH - (CONV_WIDTH - 1 - k) * BATCH
        c = c + cw[k:k + 1, :] * lru_ext[start:start + R, :]

    cb = c.astype(BF16)
    za, zx = [], []
    for p in range(LRU_HEADS // 2):
        cp = cb[:, p * 2 * LRU_HEAD_DIM:(p + 1) * 2 * LRU_HEAD_DIM]
        za.append(jnp.dot(cp, wa_ref[p], preferred_element_type=F32))
        zx.append(jnp.dot(cp, wx_ref[p], preferred_element_type=F32))
    ta = jnp.tanh(jnp.concatenate(za, axis=-1) + ba_ref[...])
    tx = jnp.tanh(jnp.concatenate(zx, axis=-1) + bx_ref[...])

    z = -lam_ref[...]
    softplus = jnp.maximum(z, 0.0) + jnp.log1p(jnp.exp(-jnp.abs(z)))
    half_rate = (-0.5 * LRU_C) * softplus
    log_a = ta * half_rate + half_rate
    a = jnp.exp(log_a)
    m2 = jnp.tanh(-log_a) * (a * a + 1.0)
    mult = m2 * lax.rsqrt(jnp.maximum(m2, F32_TINY))
    half_c = 0.5 * c
    xin = mult * (tx * half_c + half_c)

    for j in range(R // BATCH):
        rows = slice(j * BATCH, (j + 1) * BATCH)
        h = a[rows, :] * h + xin[rows, :]
        h_buf[r0 + j * BATCH:r0 + (j + 1) * BATCH, :] = h

    y_lru = h_buf[r0:r0 + R, :] * jax.nn.gelu(u_gate)
    n_pool = len(POOL_WINDOWS)
    for cs in range(LRU_WIDTH // LANES):
        y_tm[n_pool + cs, r0:r0 + R, :] = y_lru[:, cs * LANES:(cs + 1) * LANES]
    for b in range(BATCH):
        for cs in range(EVEN_MIX_WIDTH // LANES):
            y_ref[b, f0:f0 + F, cs * LANES:(cs + 1) * LANES] = (
                y_tm[cs, pl.ds(r0 + b, F, stride=BATCH), :].astype(BF16))
    return h


def _even_mix(x, w_in, pool_w, pool_scale, conv_w, conv_b, wa, wx, ba, bx, lam):
    R = MIX_ROWS
    return pl.pallas_call(
        _even_mix_kernel,
        grid=(SEQ // MIX_FRAMES,),
        in_specs=[
            pl.BlockSpec((BATCH, MIX_FRAMES, D_MODEL), lambda t: (0, t, 0)),
            _resident((D_MODEL, EVEN_IN_WIDTH)),
            _resident((len(POOL_WINDOWS), POOL_GROUP_WIDTH, POOL_GROUP_WIDTH)),
            _resident((1, POOL_WIDTH)),
            _resident((CONV_WIDTH, LRU_WIDTH)),
            _resident((1, LRU_WIDTH)),
            _resident((LRU_HEADS // 2, 2 * LRU_HEAD_DIM, 2 * LRU_HEAD_DIM)),
            _resident((LRU_HEADS // 2, 2 * LRU_HEAD_DIM, 2 * LRU_HEAD_DIM)),
            _resident((1, LRU_WIDTH)),
            _resident((1, LRU_WIDTH)),
            _resident((1, LRU_WIDTH)),
        ],
        out_specs=pl.BlockSpec((BATCH, MIX_FRAMES, EVEN_MIX_WIDTH), lambda t: (0, t, 0)),
        out_shape=jax.ShapeDtypeStruct((BATCH, SEQ, EVEN_MIX_WIDTH), BF16),
        scratch_shapes=[
            pltpu.VMEM((R + POOL_HIST * BATCH, POOL_WIDTH), F32),
            pltpu.VMEM((R + CONV_HIST * BATCH, LRU_WIDTH), F32),
            pltpu.VMEM((BATCH, LRU_WIDTH), F32),
            pltpu.VMEM((R, LRU_WIDTH), F32),
            pltpu.VMEM((D_MODEL // LANES, R, LANES), F32),
            pltpu.VMEM((EVEN_MIX_WIDTH // LANES, R, LANES), F32),
        ],
        compiler_params=pltpu.CompilerParams(
            dimension_semantics=("arbitrary",),
            vmem_limit_bytes=VMEM_LIMIT),
        name="even_mix",
    )(x, w_in, pool_w, pool_scale, conv_w, conv_b, wa, wx, ba, bx, lam)


def _out_mlp_kernel(y_ref, x_ref, w_out_ref, g1_ref, b1_ref, w1_ref, w2_ref, g2_ref, b2_ref,
                    o_ref):
    mix = jnp.dot(y_ref[...], w_out_ref[...], preferred_element_type=F32)
    x1 = _layer_norm(DEEPNORM_ALPHA * x_ref[...] + mix, g1_ref[...], b1_ref[...])
    xb = x1.astype(BF16)
    acc = None
    for c in range(D_FF // FF_CHUNK):
        lo, hi = c * FF_CHUNK, (c + 1) * FF_CHUNK
        h = jnp.dot(xb, w1_ref[:, lo:hi], preferred_element_type=F32)
        h = jnp.square(jnp.maximum(h, 0.0)).astype(BF16)
        part = jnp.dot(h, w2_ref[lo:hi, :], preferred_element_type=F32)
        acc = part if acc is None else acc + part
    o_ref[...] = _layer_norm(DEEPNORM_ALPHA * x1 + acc, g2_ref[...], b2_ref[...])


def _out_mlp(y, x, w_out, g1, b1, w1, w2, g2, b2):
    T = TOK_TILE
    kd = y.shape[-1]
    return pl.pallas_call(
        _out_mlp_kernel,
        grid=(N_TOK // T,),
        in_specs=[
            pl.BlockSpec((T, kd), lambda i: (i, 0)),
            pl.BlockSpec((T, D_MODEL), lambda i: (i, 0)),
            _resident((kd, D_MODEL)),
            _resident((1, D_MODEL)),
            _resident((1, D_MODEL)),
            _resident((D_MODEL, D_FF)),
            _resident((D_FF, D_MODEL)),
            _resident((1, D_MODEL)),
            _resident((1, D_MODEL)),
        ],
        out_specs=pl.BlockSpec((T, D_MODEL), lambda i: (i, 0)),
        out_shape=jax.ShapeDtypeStruct((N_TOK, D_MODEL), F32),
        compiler_params=pltpu.CompilerParams(
            dimension_semantics=("arbitrary",),
            vmem_limit_bytes=VMEM_LIMIT),
        name="out_mlp",
    )(y, x, w_out, g1, b1, w1, w2, g2, b2)


def _mla_proj_kernel(x_ref, pos_ref, invf_ref, w_down_ref, qg_ref, kvg_ref, w_qb_ref, w_kvb_ref,
                     q_ref, k_ref, v_ref):
    xb = x_ref[...].astype(BF16)
    down = jnp.dot(xb, w_down_ref[...], preferred_element_type=F32)
    cq = _rms_norm(down[:, :Q_LORA_RANK], qg_ref[...])
    ckv = _rms_norm(down[:, Q_LORA_RANK:Q_LORA_RANK + KV_LORA_RANK], kvg_ref[...])
    kpe = down[:, Q_LORA_RANK + KV_LORA_RANK:]

    ang = pos_ref[...].astype(F32) * invf_ref[...]
    cos = jnp.cos(ang)
    lane = lax.broadcasted_iota(jnp.int32, (1, LANES), 1)
    sin = jnp.sin(ang) * jnp.where(lane < 2 * ROPE_HALF, -1.0, 1.0)

    def rope(blk):
        return blk * cos + pltpu.roll(blk, 2 * ROPE_HALF, axis=1) * sin

    q = jnp.dot(cq.astype(BF16), w_qb_ref[...], preferred_element_type=F32)
    n_nope = MLA_HEADS * QK_NOPE_DIM
    q_pe = [rope(q[:, n_nope + p * LANES:n_nope + (p + 1) * LANES]).astype(BF16)
            for p in range(MLA_HEADS // 2)]
    kv = jnp.dot(ckv.astype(BF16), w_kvb_ref[...], preferred_element_type=F32)
    kr = rope(kpe)
    first_slot = (lane % (2 * ROPE_HALF)) < ROPE_HALF
    k_pe = [jnp.where(first_slot, kr, 0.0).astype(BF16),
            jnp.where(first_slot, 0.0, kr).astype(BF16)]
    for h in range(MLA_HEADS):
        lo = h * QK_HEAD_WIDTH
        q_ref[:, lo:lo + LANES] = q[:, h * LANES:(h + 1) * LANES].astype(BF16)
        q_ref[:, lo + LANES:lo + 2 * LANES] = q_pe[h // 2]
        k_ref[:, lo:lo + LANES] = kv[:, h * LANES:(h + 1) * LANES].astype(BF16)
        k_ref[:, lo + LANES:lo + 2 * LANES] = k_pe[h % 2]
    v_ref[...] = kv[:, n_nope:].astype(BF16)


def _mla_proj(x, pos, invf, w_down, qg, kvg, w_qb, w_kvb):
    T = TOK_TILE
    return pl.pallas_call(
        _mla_proj_kernel,
        grid=(N_TOK // T,),
        in_specs=[
            pl.BlockSpec((T, D_MODEL), lambda i: (i, 0)),
            pl.BlockSpec((T, 1), lambda i: (i, 0)),
            _resident((1, LANES)),
            _resident((D_MODEL, DOWN_WIDTH)),
            _resident((1, Q_LORA_RANK)),
            _resident((1, KV_LORA_RANK)),
            _resident((Q_LORA_RANK, QB_WIDTH)),
            _resident((KV_LORA_RANK, 2 * V_WIDTH)),
        ],
        out_specs=[
            pl.BlockSpec((T, QK_WIDTH), lambda i: (i, 0)),
            pl.BlockSpec((T, QK_WIDTH), lambda i: (i, 0)),
            pl.BlockSpec((T, V_WIDTH), lambda i: (i, 0)),
        ],
        out_shape=[
            jax.ShapeDtypeStruct((N_TOK, QK_WIDTH), BF16),
            jax.ShapeDtypeStruct((N_TOK, QK_WIDTH), BF16),
            jax.ShapeDtypeStruct((N_TOK, V_WIDTH), BF16),
        ],
        compiler_params=pltpu.CompilerParams(
            dimension_semantics=("arbitrary",),
            vmem_limit_bytes=VMEM_LIMIT),
        name="mla_proj",
    )(x, pos, invf, w_down, qg, kvg, w_qb, w_kvb)


_NT = (((1,), (1,)), ((), ()))


def _mla_attn_kernel(q_ref, k_ref, v_ref, o_ref, v_ones):
    c = (QK_NOPE_DIM + QK_ROPE_DIM) ** -0.5 * float(np.log2(np.e))
    neg = jnp.finfo(F32).min
    R = ATTN_Q_ROWS
    row = lax.broadcasted_iota(jnp.int32, (R, R), 0)
    col = lax.broadcasted_iota(jnp.int32, (R, R), 1)
    hidden = (col // CHUNK) > (row // CHUNK)
    for hh in range(ATTN_HEADS_PER_STEP):
        v_ones[hh, :, :V_HEAD_DIM] = v_ref[:, hh * V_HEAD_DIM:(hh + 1) * V_HEAD_DIM]
        v_ones[hh, :, V_HEAD_DIM:] = jnp.ones((SEQ, LANES), BF16)
    for j in range(SEQ // R):
        qs, ke = j * R, (j + 1) * R
        for hh in range(ATTN_HEADS_PER_STEP):
            lo, hi = hh * QK_HEAD_WIDTH, (hh + 1) * QK_HEAD_WIDTH
            s = lax.dot_general(q_ref[qs:ke, lo:hi], k_ref[:ke, lo:hi], _NT,
                                preferred_element_type=F32) * c
            diag = jnp.where(hidden, neg, s[:, qs:ke])
            s = diag if j == 0 else jnp.concatenate([s[:, :qs], diag], axis=-1)
            m = jnp.max(s, axis=-1, keepdims=True)
            p = jnp.exp2(s - m).astype(BF16)
            ol = jnp.dot(p, v_ones[hh, :ke, :], preferred_element_type=F32)
            o = ol[:, :V_HEAD_DIM] / ol[:, V_HEAD_DIM:V_HEAD_DIM + 1]
            o_ref[qs:ke, hh * V_HEAD_DIM:(hh + 1) * V_HEAD_DIM] = o.astype(BF16)


def _mla_attn(q, k, v):
    hps = ATTN_HEADS_PER_STEP
    blk = lambda b, g: (b, 0, g)
    return pl.pallas_call(
        _mla_attn_kernel,
        grid=(BATCH, MLA_HEADS // hps),
        in_specs=[
            pl.BlockSpec((None, SEQ, hps * QK_HEAD_WIDTH), blk),
            pl.BlockSpec((None, SEQ, hps * QK_HEAD_WIDTH), blk),
            pl.BlockSpec((None, SEQ, hps * V_HEAD_DIM), blk),
        ],
        out_specs=pl.BlockSpec((None, SEQ, hps * V_HEAD_DIM), blk),
        out_shape=jax.ShapeDtypeStruct((BATCH, SEQ, V_WIDTH), BF16),
        scratch_shapes=[pltpu.VMEM((hps, SEQ, V_HEAD_DIM + LANES), BF16)],
        compiler_params=pltpu.CompilerParams(
            dimension_semantics=("arbitrary", "arbitrary"),
            vmem_limit_bytes=VMEM_LIMIT),
        name="mla_attn",
    )(q, k, v)


def _q_weight(w_qb):
    w = w_qb.reshape(Q_LORA_RANK, MLA_HEADS, QK_NOPE_DIM + QK_ROPE_DIM)
    nope = w[:, :, :QK_NOPE_DIM].reshape(Q_LORA_RANK, MLA_HEADS * QK_NOPE_DIM)
    pe = w[:, :, QK_NOPE_DIM:].reshape(Q_LORA_RANK, MLA_HEADS // 2, 2, 2, ROPE_HALF)
    pe = pe.transpose(0, 1, 3, 2, 4).reshape(Q_LORA_RANK, (MLA_HEADS // 2) * LANES)
    return jnp.concatenate([nope, pe], axis=-1).astype(BF16)


def _kv_weight(w_kvb):
    w = w_kvb.reshape(KV_LORA_RANK, MLA_HEADS, 2, QK_NOPE_DIM)
    return w.transpose(0, 2, 1, 3).reshape(KV_LORA_RANK, 2 * V_WIDTH).astype(BF16)


def _down_weight(w_down):
    base = Q_LORA_RANK + KV_LORA_RANK
    x1 = w_down[:, base:base + ROPE_HALF]
    x2 = w_down[:, base + ROPE_HALF:]
    return jnp.concatenate([w_down[:, :base], x1, x1, x2, x2], axis=-1).astype(BF16)


def _pair_block_diag(w):
    w = w.reshape(LRU_HEADS // 2, 2, LRU_HEAD_DIM, LRU_HEAD_DIM)
    z = jnp.zeros_like(w[:, 0])
    top = jnp.concatenate([w[:, 0], z], axis=-1)
    bot = jnp.concatenate([z, w[:, 1]], axis=-1)
    return jnp.concatenate([top, bot], axis=-2)


def kernel(x, positions, ln_mix_g, ln_mix_b, ln_ffn_g, ln_ffn_b, even_w_in, pool_w, pool_scale, lru_conv_w, lru_conv_b, lru_w_a, lru_b_a, lru_w_x, lru_b_x, lru_lambda, even_w_out, mla_w_down, mla_q_norm_g, mla_kv_norm_g, mla_w_qb, mla_w_kvb, mla_w_o, mlp_w1, mlp_w2):
    row = lambda v: v.reshape(1, -1)
    inv_freq = ROPE_THETA ** (-jnp.arange(0, QK_ROPE_DIM, 2, dtype=F32) / QK_ROPE_DIM)
    invf = jnp.tile(inv_freq, LANES // ROPE_HALF).reshape(1, LANES)
    pos = positions.reshape(N_TOK, 1)

    xf = x.reshape(N_TOK, D_MODEL)
    for layer in range(DEPTH):
        j = layer // 2
        if layer % 2 == 0:
            y = _even_mix(
                xf.reshape(BATCH, SEQ, D_MODEL), even_w_in[j].astype(BF16), pool_w[j].astype(BF16),
                row(pool_scale[j]), lru_conv_w[j], row(lru_conv_b[j]),
                _pair_block_diag(0.5 * lru_w_a[j]).astype(BF16),
                _pair_block_diag(0.5 * lru_w_x[j]).astype(BF16),
                row(0.5 * lru_b_a[j]), row(0.5 * lru_b_x[j]), row(lru_lambda[j]))
            y = y.reshape(N_TOK, EVEN_MIX_WIDTH)
            w_out = even_w_out[j]
        else:
            q, k, v = _mla_proj(
                xf, pos, invf, _down_weight(mla_w_down[j]),
                row(mla_q_norm_g[j]), row(mla_kv_norm_g[j]),
                _q_weight(mla_w_qb[j]), _kv_weight(mla_w_kvb[j]))
            y = _mla_attn(q.reshape(BATCH, SEQ, QK_WIDTH), k.reshape(BATCH, SEQ, QK_WIDTH),
                          v.reshape(BATCH, SEQ, V_WIDTH)).reshape(N_TOK, V_WIDTH)
            w_out = mla_w_o[j]
        xf = _out_mlp(y, xf, w_out.astype(BF16), row(ln_mix_g[layer]), row(ln_mix_b[layer]),
                      mlp_w1[layer].astype(BF16), mlp_w2[layer].astype(BF16),
                      row(ln_ffn_g[layer]), row(ln_ffn_b[layer]))
    return xf.reshape(BATCH, SEQ, D_MODEL)
```

```python
import numpy as np
import jax
import jax.numpy as jnp
from jax import lax
from jax.experimental import pallas as pl
from jax.experimental.pallas import tpu as pltpu

D_MODEL = 1024
BATCH = 8
SEQ = 2048
DEPTH = 4
N_TOK = BATCH * SEQ

CHUNK = 64

LANES = 128
SUBLANES = 8
assert BATCH == SUBLANES

POOL_WINDOWS = (2, 4, 8, 16)
POOL_GROUP_WIDTH = 128
POOL_WIDTH = 512
POOL_HIST = 16
LRU_WIDTH = 1024
LRU_HEADS = 8
LRU_HEAD_DIM = 128
CONV_WIDTH = 4
CONV_HIST = CONV_WIDTH - 1
LRU_C = 8.0
EVEN_IN_WIDTH = POOL_WIDTH + 2 * LRU_WIDTH
EVEN_MIX_WIDTH = POOL_WIDTH + LRU_WIDTH

MLA_HEADS = 8
QK_NOPE_DIM = 128
QK_ROPE_DIM = 64
ROPE_HALF = QK_ROPE_DIM // 2
V_HEAD_DIM = 128
Q_LORA_RANK = 384
KV_LORA_RANK = 256
ROPE_THETA = 10000.0
DOWN_WIDTH = Q_LORA_RANK + KV_LORA_RANK + LANES
QB_WIDTH = MLA_HEADS * QK_NOPE_DIM + (MLA_HEADS // 2) * LANES
QK_HEAD_WIDTH = 2 * LANES
QK_WIDTH = MLA_HEADS * QK_HEAD_WIDTH
V_WIDTH = MLA_HEADS * V_HEAD_DIM
ATTN_Q_ROWS = 512
ATTN_HEADS_PER_STEP = 4

D_FF = 4 * D_MODEL
FF_CHUNK = 1024
DEEPNORM_ALPHA = (2 * DEPTH) ** 0.25
LN_EPS = 1e-5
RMS_EPS = 1e-6

MIX_FRAMES = 128
MIX_ROWS = MIX_FRAMES * BATCH
MIX_SUBTILES = 4
TOK_TILE = 512
VMEM_LIMIT = 56 * 1024 * 1024

F32 = jnp.float32
BF16 = jnp.bfloat16
F32_TINY = float(np.finfo(np.float32).tiny)


def _resident(shape):
    zeros = (0,) * len(shape)
    return pl.BlockSpec(shape, lambda *_: zeros, pipeline_mode=pl.Buffered(1))


def _layer_norm(v, g, b):
    mu = jnp.mean(v, axis=-1, keepdims=True)
    d = v - mu
    var = jnp.mean(d * d, axis=-1, keepdims=True)
    return d * lax.rsqrt(var + LN_EPS) * g + b


def _rms_norm(v, g):
    return v * lax.rsqrt(jnp.mean(v * v, axis=-1, keepdims=True) + RMS_EPS) * g


def _even_mix_kernel(x_ref, w_in_ref, pool_w_ref, pool_scale_ref, conv_w_ref, conv_b_ref,
                     wa_ref, wx_ref, ba_ref, bx_ref, lam_ref, y_ref,
                     pool_ext, lru_ext, carry, h_buf, x_tm, y_tm):
    t = pl.program_id(0)
    PH = POOL_HIST * BATCH
    CH = CONV_HIST * BATCH

    @pl.when(t == 0)
    def _():
        pool_ext[0:PH, :] = jnp.zeros((PH, POOL_WIDTH), F32)
        lru_ext[0:CH, :] = jnp.zeros((CH, LRU_WIDTH), F32)
        carry[...] = jnp.zeros((BATCH, LRU_WIDTH), F32)

    h = carry[...]
    for s in range(MIX_SUBTILES):
        h = _even_mix_rows(s, t, h, x_ref, w_in_ref, pool_w_ref, pool_scale_ref, conv_w_ref,
                           conv_b_ref, wa_ref, wx_ref, ba_ref, bx_ref, lam_ref, y_ref,
                           pool_ext, lru_ext, h_buf, x_tm, y_tm)
    carry[...] = h
    pool_ext[0:PH, :] = pool_ext[MIX_ROWS:MIX_ROWS + PH, :]
    lru_ext[0:CH, :] = lru_ext[MIX_ROWS:MIX_ROWS + CH, :]


def _even_mix_rows(s, t, h, x_ref, w_in_ref, pool_w_ref, pool_scale_ref, conv_w_ref, conv_b_ref,
                   wa_ref, wx_ref, ba_ref, bx_ref, lam_ref, y_ref, pool_ext, lru_ext, h_buf,
                   x_tm, y_tm):
    R = MIX_ROWS // MIX_SUBTILES
    F = MIX_FRAMES // MIX_SUBTILES
    r0, f0 = s * R, s * F
    PH = POOL_HIST * BATCH
    CH = CONV_HIST * BATCH

    for b in range(BATCH):
        for cs in range(D_MODEL // LANES):
            x_tm[cs, pl.ds(r0 + b, F, stride=BATCH), :] = (
                x_ref[b, f0:f0 + F, cs * LANES:(cs + 1) * LANES])
    xb = jnp.concatenate([x_tm[cs, r0:r0 + R, :] for cs in range(D_MODEL // LANES)],
                         axis=-1).astype(BF16)
    proj = jnp.dot(xb, w_in_ref[...], preferred_element_type=F32)
    u_pool = proj[:, :POOL_WIDTH]
    u_lru = proj[:, POOL_WIDTH:POOL_WIDTH + LRU_WIDTH]
    u_gate = proj[:, POOL_WIDTH + LRU_WIDTH:]
    pool_ext[PH + r0:PH + r0 + R, :] = u_pool
    lru_ext[CH + r0:CH + r0 + R, :] = u_lru

    frame = (lax.broadcasted_iota(jnp.int32, (R, 1), 0) + r0) // BATCH + t * MIX_FRAMES
    for g, w in enumerate(POOL_WINDOWS):
        lo, hi = g * POOL_GROUP_WIDTH, (g + 1) * POOL_GROUP_WIDTH
        acc = pool_ext[r0:r0 + PH + R, lo:hi]
        k = 1
        while k < w:
            acc = acc[k * BATCH:, :] + acc[:acc.shape[0] - k * BATCH, :]
            k *= 2
        acc = acc[acc.shape[0] - R:, :]
        count = jnp.minimum(frame + 1, w).astype(F32)
        d = acc / count - u_pool[:, lo:hi]
        yg = jnp.dot(d.astype(BF16), pool_w_ref[g], preferred_element_type=F32)
        y_tm[g, r0:r0 + R, :] = yg * pool_scale_ref[:, lo:hi]

    cw = conv_w_ref[...]
    c = cw[CONV_WIDTH - 1:CONV_WIDTH, :] * u_lru + conv_b_ref[...]
    for k in range(CONV_WIDTH - 1):
        start = r0 + CH - (CONV_WIDTH - 1 - k) * BATCH
        c = c + cw[k:k + 1, :] * lru_ext[start:start + R, :]

    cb = c.astype(BF16)
    za, zx = [], []
    for p in range(LRU_HEADS // 2):
        cp = cb[:, p * 2 * LRU_HEAD_DIM:(p + 1) * 2 * LRU_HEAD_DIM]
        za.append(jnp.dot(cp, wa_ref[p], preferred_element_type=F32))
        zx.append(jnp.dot(cp, wx_ref[p], preferred_element_type=F32))
    ta = jnp.tanh(jnp.concatenate(za, axis=-1) + ba_ref[...])
    tx = jnp.tanh(jnp.concatenate(zx, axis=-1) + bx_ref[...])

    z = -lam_ref[...]
    softplus = jnp.maximum(z, 0.0) + jnp.log1p(jnp.exp(-jnp.abs(z)))
    half_rate = (-0.5 * LRU_C) * softplus
    log_a = ta * half_rate + half_rate
    a = jnp.exp(log_a)
    m2 = jnp.tanh(-log_a) * (a * a + 1.0)
    mult = m2 * lax.rsqrt(jnp.maximum(m2, F32_TINY))
    half_c = 0.5 * c
    xin = mult * (tx * half_c + half_c)

    for j in range(R // BATCH):
        rows = slice(j * BATCH, (j + 1) * BATCH)
        h = a[rows, :] * h + xin[rows, :]
        h_buf[r0 + j * BATCH:r0 + (j + 1) * BATCH, :] = h

    y_lru = h_buf[r0:r0 + R, :] * jax.nn.gelu(u_gate)
    n_pool = len(POOL_WINDOWS)
    for cs in range(LRU_WIDTH // LANES):
        y_tm[n_pool + cs, r0:r0 + R, :] = y_lru[:, cs * LANES:(cs + 1) * LANES]
    for b in range(BATCH):
        for cs in range(EVEN_MIX_WIDTH // LANES):
            y_ref[b, f0:f0 + F, cs * LANES:(cs + 1) * LANES] = (
                y_tm[cs, pl.ds(r0 + b, F, stride=BATCH), :].astype(BF16))
    return h


def _even_mix(x, w_in, pool_w, pool_scale, conv_w, conv_b, wa, wx, ba, bx, lam):
    R = MIX_ROWS
    return pl.pallas_call(
        _even_mix_kernel,
        grid=(SEQ // MIX_FRAMES,),
        in_specs=[
            pl.BlockSpec((BATCH, MIX_FRAMES, D_MODEL), lambda t: (0, t, 0)),
            _resident((D_MODEL, EVEN_IN_WIDTH)),
            _resident((len(POOL_WINDOWS), POOL_GROUP_WIDTH, POOL_GROUP_WIDTH)),
            _resident((1, POOL_WIDTH)),
            _resident((CONV_WIDTH, LRU_WIDTH)),
            _resident((1, LRU_WIDTH)),
            _resident((LRU_HEADS // 2, 2 * LRU_HEAD_DIM, 2 * LRU_HEAD_DIM)),
            _resident((LRU_HEADS // 2, 2 * LRU_HEAD_DIM, 2 * LRU_HEAD_DIM)),
            _resident((1, LRU_WIDTH)),
            _resident((1, LRU_WIDTH)),
            _resident((1, LRU_WIDTH)),
        ],
        out_specs=pl.BlockSpec((BATCH, MIX_FRAMES, EVEN_MIX_WIDTH), lambda t: (0, t, 0)),
        out_shape=jax.ShapeDtypeStruct((BATCH, SEQ, EVEN_MIX_WIDTH), BF16),
        scratch_shapes=[
            pltpu.VMEM((R + POOL_HIST * BATCH, POOL_WIDTH), F32),
            pltpu.VMEM((R + CONV_HIST * BATCH, LRU_WIDTH), F32),
            pltpu.VMEM((BATCH, LRU_WIDTH), F32),
            pltpu.VMEM((R, LRU_WIDTH), F32),
            pltpu.VMEM((D_MODEL // LANES, R, LANES), F32),
            pltpu.VMEM((EVEN_MIX_WIDTH // LANES, R, LANES), F32),
        ],
        compiler_params=pltpu.CompilerParams(
            dimension_semantics=("arbitrary",),
            vmem_limit_bytes=VMEM_LIMIT),
        name="even_mix",
    )(x, w_in, pool_w, pool_scale, conv_w, conv_b, wa, wx, ba, bx, lam)


def _out_mlp_kernel(y0_ref, x0_ref, yn_ref, xn_ref, w_out_ref, g1_ref, b1_ref, w1_ref, w2_ref,
                    g2_ref, b2_ref, o_ref, x1_scr, xb_scr, pre_scr):
    i = pl.program_id(0)
    n = pl.num_programs(0) - 1

    def mix_ln(y_ref, x_ref):
        mix = jnp.dot(y_ref[...], w_out_ref[...], preferred_element_type=F32)
        return _layer_norm(DEEPNORM_ALPHA * x_ref[...] + mix, g1_ref[...], b1_ref[...])

    def final_ln():
        o_ref[...] = _layer_norm(pre_scr[...], g2_ref[...], b2_ref[...])

    @pl.when(i == 0)
    def _():
        x1 = mix_ln(y0_ref, x0_ref)
        x1_scr[...] = x1
        xb_scr[...] = x1.astype(BF16)
        pre_scr[...] = jnp.zeros(pre_scr.shape, F32)

    @pl.when(i < n)
    def _():
        final_ln()
        x1 = x1_scr[...]
        xb = xb_scr[...]
        x1n = mix_ln(yn_ref, xn_ref)
        acc = None
        for c in range(D_FF // FF_CHUNK):
            lo, hi = c * FF_CHUNK, (c + 1) * FF_CHUNK
            h = jnp.dot(xb, w1_ref[:, lo:hi], preferred_element_type=F32)
            h = jnp.square(jnp.maximum(h, 0.0)).astype(BF16)
            part = jnp.dot(h, w2_ref[lo:hi, :], preferred_element_type=F32)
            acc = part if acc is None else acc + part
        pre_scr[...] = DEEPNORM_ALPHA * x1 + acc
        x1_scr[...] = x1n
        xb_scr[...] = x1n.astype(BF16)

    @pl.when(i == n)
    def _():
        final_ln()


def _out_mlp(y, x, w_out, g1, b1, w1, w2, g2, b2):
    T = TOK_TILE
    kd = y.shape[-1]
    n = N_TOK // T
    nxt = lambda i: (jnp.minimum(i + 1, n - 1), 0)
    return pl.pallas_call(
        _out_mlp_kernel,
        grid=(n + 1,),
        in_specs=[
            pl.BlockSpec((T, kd), lambda i: (0, 0)),
            pl.BlockSpec((T, D_MODEL), lambda i: (0, 0)),
            pl.BlockSpec((T, kd), nxt),
            pl.BlockSpec((T, D_MODEL), nxt),
            _resident((kd, D_MODEL)),
            _resident((1, D_MODEL)),
            _resident((1, D_MODEL)),
            _resident((D_MODEL, D_FF)),
            _resident((D_FF, D_MODEL)),
            _resident((1, D_MODEL)),
            _resident((1, D_MODEL)),
        ],
        out_specs=pl.BlockSpec((T, D_MODEL), lambda i: (jnp.maximum(i - 1, 0), 0)),
        out_shape=jax.ShapeDtypeStruct((N_TOK, D_MODEL), F32),
        scratch_shapes=[pltpu.VMEM((T, D_MODEL), F32), pltpu.VMEM((T, D_MODEL), BF16),
                        pltpu.VMEM((T, D_MODEL), F32)],
        compiler_params=pltpu.CompilerParams(
            dimension_semantics=("arbitrary",),
            vmem_limit_bytes=VMEM_LIMIT),
        name="out_mlp",
    )(y, x, y, x, w_out, g1, b1, w1, w2, g2, b2)


def _mla_proj_kernel(x_ref, pos_ref, invf_ref, w_down_ref, qg_ref, kvg_ref, w_qb_ref, w_kvb_ref,
                     q_ref, k_ref, v_ref):
    xb = x_ref[...].astype(BF16)
    down = jnp.dot(xb, w_down_ref[...], preferred_element_type=F32)
    cq = _rms_norm(down[:, :Q_LORA_RANK], qg_ref[...])
    ckv = _rms_norm(down[:, Q_LORA_RANK:Q_LORA_RANK + KV_LORA_RANK], kvg_ref[...])
    kpe = down[:, Q_LORA_RANK + KV_LORA_RANK:]

    ang = pos_ref[...].astype(F32) * invf_ref[...]
    cos = jnp.cos(ang)
    lane = lax.broadcasted_iota(jnp.int32, (1, LANES), 1)
    sin = jnp.sin(ang) * jnp.where(lane < 2 * ROPE_HALF, -1.0, 1.0)

    def rope(blk):
        return blk * cos + pltpu.roll(blk, 2 * ROPE_HALF, axis=1) * sin

    q = jnp.dot(cq.astype(BF16), w_qb_ref[...], preferred_element_type=F32)
    n_nope = MLA_HEADS * QK_NOPE_DIM
    q_pe = [rope(q[:, n_nope + p * LANES:n_nope + (p + 1) * LANES]).astype(BF16)
            for p in range(MLA_HEADS // 2)]
    kv = jnp.dot(ckv.astype(BF16), w_kvb_ref[...], preferred_element_type=F32)
    kr = rope(kpe)
    first_slot = (lane % (2 * ROPE_HALF)) < ROPE_HALF
    k_pe = [jnp.where(first_slot, kr, 0.0).astype(BF16),
            jnp.where(first_slot, 0.0, kr).astype(BF16)]
    for h in range(MLA_HEADS):
        lo = h * QK_HEAD_WIDTH
        q_ref[:, lo:lo + LANES] = q[:, h * LANES:(h + 1) * LANES].astype(BF16)
        q_ref[:, lo + LANES:lo + 2 * LANES] = q_pe[h // 2]
        k_ref[:, lo:lo + LANES] = kv[:, h * LANES:(h + 1) * LANES].astype(BF16)
        k_ref[:, lo + LANES:lo + 2 * LANES] = k_pe[h % 2]
    v_ref[...] = kv[:, n_nope:].astype(BF16)


def _mla_proj(x, pos, invf, w_down, qg, kvg, w_qb, w_kvb):
    T = TOK_TILE
    return pl.pallas_call(
        _mla_proj_kernel,
        grid=(N_TOK // T,),
        in_specs=[
            pl.BlockSpec((T, D_MODEL), lambda i: (i, 0)),
            pl.BlockSpec((T, 1), lambda i: (i, 0)),
            _resident((1, LANES)),
            _resident((D_MODEL, DOWN_WIDTH)),
            _resident((1, Q_LORA_RANK)),
            _resident((1, KV_LORA_RANK)),
            _resident((Q_LORA_RANK, QB_WIDTH)),
            _resident((KV_LORA_RANK, 2 * V_WIDTH)),
        ],
        out_specs=[
            pl.BlockSpec((T, QK_WIDTH), lambda i: (i, 0)),
            pl.BlockSpec((T, QK_WIDTH), lambda i: (i, 0)),
            pl.BlockSpec((T, V_WIDTH), lambda i: (i, 0)),
        ],
        out_shape=[
            jax.ShapeDtypeStruct((N_TOK, QK_WIDTH), BF16),
            jax.ShapeDtypeStruct((N_TOK, QK_WIDTH), BF16),
            jax.ShapeDtypeStruct((N_TOK, V_WIDTH), BF16),
        ],
        compiler_params=pltpu.CompilerParams(
            dimension_semantics=("arbitrary",),
            vmem_limit_bytes=VMEM_LIMIT),
        name="mla_proj",
    )(x, pos, invf, w_down, qg, kvg, w_qb, w_kvb)


_NT = (((1,), (1,)), ((), ()))


def _mla_attn_kernel(q_ref, k_ref, v_ref, o_ref, v_ones):
    c = (QK_NOPE_DIM + QK_ROPE_DIM) ** -0.5 * float(np.log2(np.e))
    neg = jnp.finfo(F32).min
    R = ATTN_Q_ROWS
    row = lax.broadcasted_iota(jnp.int32, (R, R), 0)
    col = lax.broadcasted_iota(jnp.int32, (R, R), 1)
    hidden = (col // CHUNK) > (row // CHUNK)
    for hh in range(ATTN_HEADS_PER_STEP):
        v_ones[hh, :, :V_HEAD_DIM] = v_ref[:, hh * V_HEAD_DIM:(hh + 1) * V_HEAD_DIM]
        v_ones[hh, :, V_HEAD_DIM:] = jnp.ones((SEQ, LANES), BF16)
    for j in range(SEQ // R):
        qs, ke = j * R, (j + 1) * R
        for hh in range(ATTN_HEADS_PER_STEP):
            lo, hi = hh * QK_HEAD_WIDTH, (hh + 1) * QK_HEAD_WIDTH
            s = lax.dot_general(q_ref[qs:ke, lo:hi], k_ref[:ke, lo:hi], _NT,
                                preferred_element_type=F32) * c
            diag = jnp.where(hidden, neg, s[:, qs:ke])
            s = diag if j == 0 else jnp.concatenate([s[:, :qs], diag], axis=-1)
            m = jnp.max(s, axis=-1, keepdims=True)
            p = jnp.exp2(s - m).astype(BF16)
            ol = jnp.dot(p, v_ones[hh, :ke, :], preferred_element_type=F32)
            o = ol[:, :V_HEAD_DIM] / ol[:, V_HEAD_DIM:V_HEAD_DIM + 1]
            o_ref[qs:ke, hh * V_HEAD_DIM:(hh + 1) * V_HEAD_DIM] = o.astype(BF16)


def _mla_attn(q, k, v):
    hps = ATTN_HEADS_PER_STEP
    blk = lambda b, g: (b, 0, g)
    return pl.pallas_call(
        _mla_attn_kernel,
        grid=(BATCH, MLA_HEADS // hps),
        in_specs=[
            pl.BlockSpec((None, SEQ, hps * QK_HEAD_WIDTH), blk),
            pl.BlockSpec((None, SEQ, hps * QK_HEAD_WIDTH), blk),
            pl.BlockSpec((None, SEQ, hps * V_HEAD_DIM), blk),
        ],
        out_specs=pl.BlockSpec((None, SEQ, hps * V_HEAD_DIM), blk),
        out_shape=jax.ShapeDtypeStruct((BATCH, SEQ, V_WIDTH), BF16),
        scratch_shapes=[pltpu.VMEM((hps, SEQ, V_HEAD_DIM + LANES), BF16)],
        compiler_params=pltpu.CompilerParams(
            dimension_semantics=("arbitrary", "arbitrary"),
            vmem_limit_bytes=VMEM_LIMIT),
        name="mla_attn",
    )(q, k, v)


def _q_weight(w_qb):
    w = w_qb.reshape(Q_LORA_RANK, MLA_HEADS, QK_NOPE_DIM + QK_ROPE_DIM)
    nope = w[:, :, :QK_NOPE_DIM].reshape(Q_LORA_RANK, MLA_HEADS * QK_NOPE_DIM)
    pe = w[:, :, QK_NOPE_DIM:].reshape(Q_LORA_RANK, MLA_HEADS // 2, 2, 2, ROPE_HALF)
    pe = pe.transpose(0, 1, 3, 2, 4).reshape(Q_LORA_RANK, (MLA_HEADS // 2) * LANES)
    return jnp.concatenate([nope, pe], axis=-1).astype(BF16)


def _kv_weight(w_kvb):
    w = w_kvb.reshape(KV_LORA_RANK, MLA_HEADS, 2, QK_NOPE_DIM)
    return w.transpose(0, 2, 1, 3).reshape(KV_LORA_RANK, 2 * V_WIDTH).astype(BF16)


def _down_weight(w_down):
    base = Q_LORA_RANK + KV_LORA_RANK
    x1 = w_down[:, base:base + ROPE_HALF]
    x2 = w_down[:, base + ROPE_HALF:]
    return jnp.concatenate([w_down[:, :base], x1, x1, x2, x2], axis=-1).astype(BF16)


def _pair_block_diag(w):
    w = w.reshape(LRU_HEADS // 2, 2, LRU_HEAD_DIM, LRU_HEAD_DIM)
    z = jnp.zeros_like(w[:, 0])
    top = jnp.concatenate([w[:, 0], z], axis=-1)
    bot = jnp.concatenate([z, w[:, 1]], axis=-1)
    return jnp.concatenate([top, bot], axis=-2)


def kernel(x, positions, ln_mix_g, ln_mix_b, ln_ffn_g, ln_ffn_b, even_w_in, pool_w, pool_scale, lru_conv_w, lru_conv_b, lru_w_a, lru_b_a, lru_w_x, lru_b_x, lru_lambda, even_w_out, mla_w_down, mla_q_norm_g, mla_kv_norm_g, mla_w_qb, mla_w_kvb, mla_w_o, mlp_w1, mlp_w2):
    row = lambda v: v.reshape(1, -1)
    inv_freq = ROPE_THETA ** (-jnp.arange(0, QK_ROPE_DIM, 2, dtype=F32) / QK_ROPE_DIM)
    invf = jnp.tile(inv_freq, LANES // ROPE_HALF).reshape(1, LANES)
    pos = positions.reshape(N_TOK, 1)

    xf = x.reshape(N_TOK, D_MODEL)
    for layer in range(DEPTH):
        j = layer // 2
        if layer % 2 == 0:
            y = _even_mix(
                xf.reshape(BATCH, SEQ, D_MODEL), even_w_in[j].astype(BF16), pool_w[j].astype(BF16),
                row(pool_scale[j]), lru_conv_w[j], row(lru_conv_b[j]),
                _pair_block_diag(0.5 * lru_w_a[j]).astype(BF16),
                _pair_block_diag(0.5 * lru_w_x[j]).astype(BF16),
                row(0.5 * lru_b_a[j]), row(0.5 * lru_b_x[j]), row(lru_lambda[j]))
            y = y.reshape(N_TOK, EVEN_MIX_WIDTH)
            w_out = even_w_out[j]
        else:
            q, k, v = _mla_proj(
                xf, pos, invf, _down_weight(mla_w_down[j]),
                row(mla_q_norm_g[j]), row(mla_kv_norm_g[j]),
                _q_weight(mla_w_qb[j]), _kv_weight(mla_w_kvb[j]))
            y = _mla_attn(q.reshape(BATCH, SEQ, QK_WIDTH), k.reshape(BATCH, SEQ, QK_WIDTH),
                          v.reshape(BATCH, SEQ, V_WIDTH)).reshape(N_TOK, V_WIDTH)
            w_out = mla_w_o[j]
        xf = _out_mlp(y, xf, w_out.astype(BF16), row(ln_mix_g[layer]), row(ln_mix_b[layer]),
                      mlp_w1[layer].astype(BF16), mlp_w2[layer].astype(BF16),
                      row(ln_ffn_g[layer]), row(ln_ffn_b[layer]))
    return xf.reshape(BATCH, SEQ, D_MODEL)
```

```python
import numpy as np
import jax
import jax.numpy as jnp
from jax import lax
from jax.experimental import pallas as pl
from jax.experimental.pallas import tpu as pltpu

D_MODEL = 1024
BATCH = 8
SEQ = 2048
DEPTH = 4
N_TOK = BATCH * SEQ

CHUNK = 64

LANES = 128
SUBLANES = 8
assert BATCH == SUBLANES

POOL_WINDOWS = (2, 4, 8, 16)
POOL_GROUP_WIDTH = 128
POOL_WIDTH = 512
POOL_HIST = 16
LRU_WIDTH = 1024
LRU_HEADS = 8
LRU_HEAD_DIM = 128
CONV_WIDTH = 4
CONV_HIST = CONV_WIDTH - 1
LRU_C = 8.0
EVEN_IN_WIDTH = POOL_WIDTH + 2 * LRU_WIDTH
EVEN_MIX_WIDTH = POOL_WIDTH + LRU_WIDTH

MLA_HEADS = 8
QK_NOPE_DIM = 128
QK_ROPE_DIM = 64
ROPE_HALF = QK_ROPE_DIM // 2
V_HEAD_DIM = 128
Q_LORA_RANK = 384
KV_LORA_RANK = 256
ROPE_THETA = 10000.0
DOWN_WIDTH = Q_LORA_RANK + KV_LORA_RANK + LANES
QB_WIDTH = MLA_HEADS * QK_NOPE_DIM + (MLA_HEADS // 2) * LANES
QK_HEAD_WIDTH = 2 * LANES
QK_WIDTH = MLA_HEADS * QK_HEAD_WIDTH
V_WIDTH = MLA_HEADS * V_HEAD_DIM
ATTN_Q_ROWS = 512
ONES_ROWS = 16
ATTN_HEADS_PER_STEP = 4

D_FF = 4 * D_MODEL
FF_CHUNK = 1024
DEEPNORM_ALPHA = (2 * DEPTH) ** 0.25
LN_EPS = 1e-5
RMS_EPS = 1e-6

MIX_FRAMES = 64
MIX_ROWS = MIX_FRAMES * BATCH
MIX_SUBTILES = 2
TOK_TILE = 512
VMEM_LIMIT = 56 * 1024 * 1024

F32 = jnp.float32
BF16 = jnp.bfloat16
F32_TINY = float(np.finfo(np.float32).tiny)


def _resident(shape):
    zeros = (0,) * len(shape)
    return pl.BlockSpec(shape, lambda *_: zeros, pipeline_mode=pl.Buffered(1))


def _layer_norm(v, g, b):
    mu = jnp.mean(v, axis=-1, keepdims=True)
    d = v - mu
    var = jnp.mean(d * d, axis=-1, keepdims=True)
    return d * lax.rsqrt(var + LN_EPS) * g + b


def _rms_norm(v, g):
    return v * lax.rsqrt(jnp.mean(v * v, axis=-1, keepdims=True) + RMS_EPS) * g


def _even_mix_kernel(x_ref, w_in_ref, pool_w_ref, pool_scale_ref, conv_w_ref, conv_b_ref,
                     wa_ref, wx_ref, ba_ref, bx_ref, lam_ref, y_ref,
                     pool_ext, lru_ext, carry, h_buf, x_tm, y_tm):
    t = pl.program_id(0)
    PH = POOL_HIST * BATCH
    CH = CONV_HIST * BATCH

    @pl.when(t == 0)
    def _():
        pool_ext[0:PH, :] = jnp.zeros((PH, POOL_WIDTH), F32)
        lru_ext[0:CH, :] = jnp.zeros((CH, LRU_WIDTH), F32)
        carry[...] = jnp.zeros((BATCH, LRU_WIDTH), F32)

    h = carry[...]
    for s in range(MIX_SUBTILES):
        h = _even_mix_rows(s, t, h, x_ref, w_in_ref, pool_w_ref, pool_scale_ref, conv_w_ref,
                           conv_b_ref, wa_ref, wx_ref, ba_ref, bx_ref, lam_ref, y_ref,
                           pool_ext, lru_ext, h_buf, x_tm, y_tm)
    carry[...] = h
    pool_ext[0:PH, :] = pool_ext[MIX_ROWS:MIX_ROWS + PH, :]
    lru_ext[0:CH, :] = lru_ext[MIX_ROWS:MIX_ROWS + CH, :]


def _even_mix_rows(s, t, h, x_ref, w_in_ref, pool_w_ref, pool_scale_ref, conv_w_ref, conv_b_ref,
                   wa_ref, wx_ref, ba_ref, bx_ref, lam_ref, y_ref, pool_ext, lru_ext, h_buf,
                   x_tm, y_tm):
    R = MIX_ROWS // MIX_SUBTILES
    F = MIX_FRAMES // MIX_SUBTILES
    r0, f0 = s * R, s * F
    PH = POOL_HIST * BATCH
    CH = CONV_HIST * BATCH

    for b in range(BATCH):
        for cs in range(D_MODEL // LANES):
            x_tm[cs, pl.ds(r0 + b, F, stride=BATCH), :] = (
                x_ref[b, f0:f0 + F, cs * LANES:(cs + 1) * LANES])
    xb = jnp.concatenate([x_tm[cs, r0:r0 + R, :] for cs in range(D_MODEL // LANES)],
                         axis=-1).astype(BF16)
    proj = jnp.dot(xb, w_in_ref[...], preferred_element_type=F32)
    u_pool = proj[:, :POOL_WIDTH]
    u_lru = proj[:, POOL_WIDTH:POOL_WIDTH + LRU_WIDTH]
    u_gate = proj[:, POOL_WIDTH + LRU_WIDTH:]
    pool_ext[PH + r0:PH + r0 + R, :] = u_pool
    lru_ext[CH + r0:CH + r0 + R, :] = u_lru

    frame = (lax.broadcasted_iota(jnp.int32, (R, 1), 0) + r0) // BATCH + t * MIX_FRAMES
    for g, w in enumerate(POOL_WINDOWS):
        lo, hi = g * POOL_GROUP_WIDTH, (g + 1) * POOL_GROUP_WIDTH
        acc = pool_ext[r0:r0 + PH + R, lo:hi]
        k = 1
        while k < w:
            acc = acc[k * BATCH:, :] + acc[:acc.shape[0] - k * BATCH, :]
            k *= 2
        acc = acc[acc.shape[0] - R:, :]
        count = jnp.minimum(frame + 1, w).astype(F32)
        d = acc / count - u_pool[:, lo:hi]
        yg = jnp.dot(d.astype(BF16), pool_w_ref[g], preferred_element_type=F32)
        y_tm[g, r0:r0 + R, :] = yg * pool_scale_ref[:, lo:hi]

    cw = conv_w_ref[...]
    c = cw[CONV_WIDTH - 1:CONV_WIDTH, :] * u_lru + conv_b_ref[...]
    for k in range(CONV_WIDTH - 1):
        start = r0 + CH - (CONV_WIDTH - 1 - k) * BATCH
        c = c + cw[k:k + 1, :] * lru_ext[start:start + R, :]

    cb = c.astype(BF16)
    za, zx = [], []
    for p in range(LRU_HEADS // 2):
        cp = cb[:, p * 2 * LRU_HEAD_DIM:(p + 1) * 2 * LRU_HEAD_DIM]
        za.append(jnp.dot(cp, wa_ref[p], preferred_element_type=F32))
        zx.append(jnp.dot(cp, wx_ref[p], preferred_element_type=F32))
    ta = jnp.tanh(jnp.concatenate(za, axis=-1) + ba_ref[...])
    tx = jnp.tanh(jnp.concatenate(zx, axis=-1) + bx_ref[...])

    z = -lam_ref[...]
    softplus = jnp.maximum(z, 0.0) + jnp.log1p(jnp.exp(-jnp.abs(z)))
    half_rate = (-0.5 * LRU_C) * softplus
    log_a = ta * half_rate + half_rate
    a = jnp.exp(log_a)
    m2 = jnp.tanh(-log_a) * (a * a + 1.0)
    mult = m2 * lax.rsqrt(jnp.maximum(m2, F32_TINY))
    half_c = 0.5 * c
    xin = mult * (tx * half_c + half_c)

    for j in range(R // BATCH):
        rows = slice(j * BATCH, (j + 1) * BATCH)
        h = a[rows, :] * h + xin[rows, :]
        h_buf[r0 + j * BATCH:r0 + (j + 1) * BATCH, :] = h

    y_lru = h_buf[r0:r0 + R, :] * jax.nn.gelu(u_gate)
    n_pool = len(POOL_WINDOWS)
    for cs in range(LRU_WIDTH // LANES):
        y_tm[n_pool + cs, r0:r0 + R, :] = y_lru[:, cs * LANES:(cs + 1) * LANES]
    for b in range(BATCH):
        for cs in range(EVEN_MIX_WIDTH // LANES):
            y_ref[b, f0:f0 + F, cs * LANES:(cs + 1) * LANES] = (
                y_tm[cs, pl.ds(r0 + b, F, stride=BATCH), :].astype(BF16))
    return h


def _even_mix(x, w_in, pool_w, pool_scale, conv_w, conv_b, wa, wx, ba, bx, lam):
    R = MIX_ROWS
    return pl.pallas_call(
        _even_mix_kernel,
        grid=(SEQ // MIX_FRAMES,),
        in_specs=[
            pl.BlockSpec((BATCH, MIX_FRAMES, D_MODEL), lambda t: (0, t, 0)),
            _resident((D_MODEL, EVEN_IN_WIDTH)),
            _resident((len(POOL_WINDOWS), POOL_GROUP_WIDTH, POOL_GROUP_WIDTH)),
            _resident((1, POOL_WIDTH)),
            _resident((CONV_WIDTH, LRU_WIDTH)),
            _resident((1, LRU_WIDTH)),
            _resident((LRU_HEADS // 2, 2 * LRU_HEAD_DIM, 2 * LRU_HEAD_DIM)),
            _resident((LRU_HEADS // 2, 2 * LRU_HEAD_DIM, 2 * LRU_HEAD_DIM)),
            _resident((1, LRU_WIDTH)),
            _resident((1, LRU_WIDTH)),
            _resident((1, LRU_WIDTH)),
        ],
        out_specs=pl.BlockSpec((BATCH, MIX_FRAMES, EVEN_MIX_WIDTH), lambda t: (0, t, 0)),
        out_shape=jax.ShapeDtypeStruct((BATCH, SEQ, EVEN_MIX_WIDTH), BF16),
        scratch_shapes=[
            pltpu.VMEM((R + POOL_HIST * BATCH, POOL_WIDTH), F32),
            pltpu.VMEM((R + CONV_HIST * BATCH, LRU_WIDTH), F32),
            pltpu.VMEM((BATCH, LRU_WIDTH), F32),
            pltpu.VMEM((R, LRU_WIDTH), F32),
            pltpu.VMEM((D_MODEL // LANES, R, LANES), F32),
            pltpu.VMEM((EVEN_MIX_WIDTH // LANES, R, LANES), F32),
        ],
        compiler_params=pltpu.CompilerParams(
            dimension_semantics=("arbitrary",),
            vmem_limit_bytes=VMEM_LIMIT),
        name="even_mix",
    )(x, w_in, pool_w, pool_scale, conv_w, conv_b, wa, wx, ba, bx, lam)


def _out_mlp_kernel(y_ref, x_ref, w_out_ref, g1_ref, b1_ref, w1_ref, w2_ref, g2_ref, b2_ref,
                    o_ref):
    mix = jnp.dot(y_ref[...], w_out_ref[...], preferred_element_type=F32)
    x1 = _layer_norm(DEEPNORM_ALPHA * x_ref[...] + mix, g1_ref[...], b1_ref[...])
    xb = x1.astype(BF16)
    acc = None
    for c in range(D_FF // FF_CHUNK):
        lo, hi = c * FF_CHUNK, (c + 1) * FF_CHUNK
        h = jnp.dot(xb, w1_ref[:, lo:hi], preferred_element_type=F32)
        h = jnp.square(jnp.maximum(h, 0.0)).astype(BF16)
        part = jnp.dot(h, w2_ref[lo:hi, :], preferred_element_type=F32)
        acc = part if acc is None else acc + part
    o_ref[...] = _layer_norm(DEEPNORM_ALPHA * x1 + acc, g2_ref[...], b2_ref[...])


def _out_mlp(y, x, w_out, g1, b1, w1, w2, g2, b2):
    T = TOK_TILE
    kd = y.shape[-1]
    return pl.pallas_call(
        _out_mlp_kernel,
        grid=(N_TOK // T,),
        in_specs=[
            pl.BlockSpec((T, kd), lambda i: (i, 0)),
            pl.BlockSpec((T, D_MODEL), lambda i: (i, 0)),
            _resident((kd, D_MODEL)),
            _resident((1, D_MODEL)),
            _resident((1, D_MODEL)),
            _resident((D_MODEL, D_FF)),
            _resident((D_FF, D_MODEL)),
            _resident((1, D_MODEL)),
            _resident((1, D_MODEL)),
        ],
        out_specs=pl.BlockSpec((T, D_MODEL), lambda i: (i, 0)),
        out_shape=jax.ShapeDtypeStruct((N_TOK, D_MODEL), F32),
        compiler_params=pltpu.CompilerParams(
            dimension_semantics=("arbitrary",),
            vmem_limit_bytes=VMEM_LIMIT),
        name="out_mlp",
    )(y, x, w_out, g1, b1, w1, w2, g2, b2)


def _mla_proj_kernel(x_ref, pos_ref, invf_ref, w_down_ref, qg_ref, kvg_ref, w_qb_ref, w_kb_ref,
                     w_vbt_ref, q_ref, k_ref, vt_ref):
    xb = x_ref[...].astype(BF16)
    down = jnp.dot(xb, w_down_ref[...], preferred_element_type=F32)
    cq = _rms_norm(down[:, :Q_LORA_RANK], qg_ref[...])
    ckv = _rms_norm(down[:, Q_LORA_RANK:Q_LORA_RANK + KV_LORA_RANK], kvg_ref[...])
    kpe = down[:, Q_LORA_RANK + KV_LORA_RANK:]

    ang = pos_ref[...].astype(F32) * invf_ref[...]
    cos = jnp.cos(ang)
    lane = lax.broadcasted_iota(jnp.int32, (1, LANES), 1)
    sin = jnp.sin(ang) * jnp.where(lane < 2 * ROPE_HALF, -1.0, 1.0)

    def rope(blk):
        return blk * cos + pltpu.roll(blk, 2 * ROPE_HALF, axis=1) * sin

    q = jnp.dot(cq.astype(BF16), w_qb_ref[...], preferred_element_type=F32)
    n_nope = MLA_HEADS * QK_NOPE_DIM
    q_pe = [rope(q[:, n_nope + p * LANES:n_nope + (p + 1) * LANES]).astype(BF16)
            for p in range(MLA_HEADS // 2)]
    ckv_b = ckv.astype(BF16)
    kn = jnp.dot(ckv_b, w_kb_ref[...], preferred_element_type=F32)
    vt_ref[...] = lax.dot_general(w_vbt_ref[...], ckv_b, _NT,
                                  preferred_element_type=F32).astype(BF16)
    kr = rope(kpe)
    first_slot = (lane % (2 * ROPE_HALF)) < ROPE_HALF
    k_pe = [jnp.where(first_slot, kr, 0.0).astype(BF16),
            jnp.where(first_slot, 0.0, kr).astype(BF16)]
    for h in range(MLA_HEADS):
        lo = h * QK_HEAD_WIDTH
        q_ref[:, lo:lo + LANES] = q[:, h * LANES:(h + 1) * LANES].astype(BF16)
        q_ref[:, lo + LANES:lo + 2 * LANES] = q_pe[h // 2]
        k_ref[:, lo:lo + LANES] = kn[:, h * LANES:(h + 1) * LANES].astype(BF16)
        k_ref[:, lo + LANES:lo + 2 * LANES] = k_pe[h % 2]


def _mla_proj(x, pos, invf, w_down, qg, kvg, w_qb, w_kb, w_vbt):
    T = TOK_TILE
    tiles_per_stream = SEQ // T
    return pl.pallas_call(
        _mla_proj_kernel,
        grid=(N_TOK // T,),
        in_specs=[
            pl.BlockSpec((T, D_MODEL), lambda i: (i, 0)),
            pl.BlockSpec((T, 1), lambda i: (i, 0)),
            _resident((1, LANES)),
            _resident((D_MODEL, DOWN_WIDTH)),
            _resident((1, Q_LORA_RANK)),
            _resident((1, KV_LORA_RANK)),
            _resident((Q_LORA_RANK, QB_WIDTH)),
            _resident((KV_LORA_RANK, V_WIDTH)),
            _resident((V_WIDTH, KV_LORA_RANK)),
        ],
        out_specs=[
            pl.BlockSpec((T, QK_WIDTH), lambda i: (i, 0)),
            pl.BlockSpec((T, QK_WIDTH), lambda i: (i, 0)),
            pl.BlockSpec((None, V_WIDTH, T),
                         lambda i: (i // tiles_per_stream, 0, i % tiles_per_stream)),
        ],
        out_shape=[
            jax.ShapeDtypeStruct((N_TOK, QK_WIDTH), BF16),
            jax.ShapeDtypeStruct((N_TOK, QK_WIDTH), BF16),
            jax.ShapeDtypeStruct((BATCH, V_WIDTH, SEQ), BF16),
        ],
        compiler_params=pltpu.CompilerParams(
            dimension_semantics=("arbitrary",),
            vmem_limit_bytes=VMEM_LIMIT),
        name="mla_proj",
    )(x, pos, invf, w_down, qg, kvg, w_qb, w_kb, w_vbt)


_NT = (((1,), (1,)), ((), ()))


def _mla_attn_kernel(q_ref, k_ref, vt_ref, o_ref, vt_ones):
    c = (QK_NOPE_DIM + QK_ROPE_DIM) ** -0.5 * float(np.log2(np.e))
    neg = jnp.finfo(F32).min
    R = ATTN_Q_ROWS
    key = lax.broadcasted_iota(jnp.int32, (R, R), 0)
    qry = lax.broadcasted_iota(jnp.int32, (R, R), 1)
    hidden = (key // CHUNK) > (qry // CHUNK)
    for hh in range(ATTN_HEADS_PER_STEP):
        vt_ones[hh, :V_HEAD_DIM, :] = vt_ref[hh * V_HEAD_DIM:(hh + 1) * V_HEAD_DIM, :]
        vt_ones[hh, V_HEAD_DIM:, :] = jnp.ones((ONES_ROWS, SEQ), BF16)
    def scores(j, hh):
        qs, ke = j * R, (j + 1) * R
        lo, hi = hh * QK_HEAD_WIDTH, (hh + 1) * QK_HEAD_WIDTH
        return lax.dot_general(k_ref[:ke, lo:hi], q_ref[qs:ke, lo:hi], _NT,
                               preferred_element_type=F32) * c

    work = [(j, hh) for j in range(SEQ // R) for hh in range(ATTN_HEADS_PER_STEP)]
    st_next = scores(*work[0])
    for t, (j, hh) in enumerate(work):
        qs, ke = j * R, (j + 1) * R
        st = st_next
        if t + 1 < len(work):
            st_next = scores(*work[t + 1])
        diag = jnp.where(hidden, neg, st[qs:ke, :])
        st = diag if j == 0 else jnp.concatenate([st[:qs, :], diag], axis=0)
        m = jnp.max(st, axis=0, keepdims=True)
        pt = jnp.exp2(st - m).astype(BF16)
        olt = jnp.dot(vt_ones[hh, :, :ke], pt, preferred_element_type=F32)
        ot = olt[:V_HEAD_DIM, :] / olt[V_HEAD_DIM:V_HEAD_DIM + 1, :]
        o_ref[qs:ke, hh * V_HEAD_DIM:(hh + 1) * V_HEAD_DIM] = ot.T.astype(BF16)


def _mla_attn(q, k, vt):
    hps = ATTN_HEADS_PER_STEP
    blk = lambda b, g: (b, 0, g)
    return pl.pallas_call(
        _mla_attn_kernel,
        grid=(BATCH, MLA_HEADS // hps),
        in_specs=[
            pl.BlockSpec((None, SEQ, hps * QK_HEAD_WIDTH), blk),
            pl.BlockSpec((None, SEQ, hps * QK_HEAD_WIDTH), blk),
            pl.BlockSpec((None, hps * V_HEAD_DIM, SEQ), lambda b, g: (b, g, 0)),
        ],
        out_specs=pl.BlockSpec((None, SEQ, hps * V_HEAD_DIM), blk),
        out_shape=jax.ShapeDtypeStruct((BATCH, SEQ, V_WIDTH), BF16),
        scratch_shapes=[pltpu.VMEM((hps, V_HEAD_DIM + ONES_ROWS, SEQ), BF16)],
        compiler_params=pltpu.CompilerParams(
            dimension_semantics=("arbitrary", "arbitrary"),
            vmem_limit_bytes=VMEM_LIMIT),
        name="mla_attn",
    )(q, k, vt)


def _q_weight(w_qb):
    w = w_qb.reshape(Q_LORA_RANK, MLA_HEADS, QK_NOPE_DIM + QK_ROPE_DIM)
    nope = w[:, :, :QK_NOPE_DIM].reshape(Q_LORA_RANK, MLA_HEADS * QK_NOPE_DIM)
    pe = w[:, :, QK_NOPE_DIM:].reshape(Q_LORA_RANK, MLA_HEADS // 2, 2, 2, ROPE_HALF)
    pe = pe.transpose(0, 1, 3, 2, 4).reshape(Q_LORA_RANK, (MLA_HEADS // 2) * LANES)
    return jnp.concatenate([nope, pe], axis=-1).astype(BF16)


def _kv_weights(w_kvb):
    w = w_kvb.reshape(KV_LORA_RANK, MLA_HEADS, 2, QK_NOPE_DIM)
    w_kb = w[:, :, 0, :].reshape(KV_LORA_RANK, V_WIDTH)
    w_vbt = w[:, :, 1, :].reshape(KV_LORA_RANK, V_WIDTH).T
    return w_kb.astype(BF16), w_vbt.astype(BF16)


def _down_weight(w_down):
    base = Q_LORA_RANK + KV_LORA_RANK
    x1 = w_down[:, base:base + ROPE_HALF]
    x2 = w_down[:, base + ROPE_HALF:]
    return jnp.concatenate([w_down[:, :base], x1, x1, x2, x2], axis=-1).astype(BF16)


def _pair_block_diag(w):
    w = w.reshape(LRU_HEADS // 2, 2, LRU_HEAD_DIM, LRU_HEAD_DIM)
    z = jnp.zeros_like(w[:, 0])
    top = jnp.concatenate([w[:, 0], z], axis=-1)
    bot = jnp.concatenate([z, w[:, 1]], axis=-1)
    return jnp.concatenate([top, bot], axis=-2)


def kernel(x, positions, ln_mix_g, ln_mix_b, ln_ffn_g, ln_ffn_b, even_w_in, pool_w, pool_scale, lru_conv_w, lru_conv_b, lru_w_a, lru_b_a, lru_w_x, lru_b_x, lru_lambda, even_w_out, mla_w_down, mla_q_norm_g, mla_kv_norm_g, mla_w_qb, mla_w_kvb, mla_w_o, mlp_w1, mlp_w2):
    row = lambda v: v.reshape(1, -1)
    inv_freq = ROPE_THETA ** (-jnp.arange(0, QK_ROPE_DIM, 2, dtype=F32) / QK_ROPE_DIM)
    invf = jnp.tile(inv_freq, LANES // ROPE_HALF).reshape(1, LANES)
    pos = positions.reshape(N_TOK, 1)

    xf = x.reshape(N_TOK, D_MODEL)
    for layer in range(DEPTH):
        j = layer // 2
        if layer % 2 == 0:
            y = _even_mix(
                xf.reshape(BATCH, SEQ, D_MODEL), even_w_in[j].astype(BF16), pool_w[j].astype(BF16),
                row(pool_scale[j]), lru_conv_w[j], row(lru_conv_b[j]),
                _pair_block_diag(0.5 * lru_w_a[j]).astype(BF16),
                _pair_block_diag(0.5 * lru_w_x[j]).astype(BF16),
                row(0.5 * lru_b_a[j]), row(0.5 * lru_b_x[j]), row(lru_lambda[j]))
            y = y.reshape(N_TOK, EVEN_MIX_WIDTH)
            w_out = even_w_out[j]
        else:
            q, k, v = _mla_proj(
                xf, pos, invf, _down_weight(mla_w_down[j]),
                row(mla_q_norm_g[j]), row(mla_kv_norm_g[j]),
                _q_weight(mla_w_qb[j]), *_kv_weights(mla_w_kvb[j]))
            y = _mla_attn(q.reshape(BATCH, SEQ, QK_WIDTH), k.reshape(BATCH, SEQ, QK_WIDTH),
                          v).reshape(N_TOK, V_WIDTH)
            w_out = mla_w_o[j]
        xf = _out_mlp(y, xf, w_out.astype(BF16), row(ln_mix_g[layer]), row(ln_mix_b[layer]),
                      mlp_w1[layer].astype(BF16), mlp_w2[layer].astype(BF16),
                      row(ln_ffn_g[layer]), row(ln_ffn_b[layer]))
    return xf.reshape(BATCH, SEQ, D_MODEL)
```

```python
import numpy as np
import jax
import jax.numpy as jnp
from jax import lax
from jax.experimental import pallas as pl
from jax.experimental.pallas import tpu as pltpu

D_MODEL = 1024
BATCH = 8
SEQ = 2048
DEPTH = 4
N_TOK = BATCH * SEQ

CHUNK = 64

LANES = 128
SUBLANES = 8
assert BATCH == SUBLANES

POOL_WINDOWS = (2, 4, 8, 16)
POOL_GROUP_WIDTH = 128
POOL_WIDTH = 512
POOL_HIST = 16
LRU_WIDTH = 1024
LRU_HEADS = 8
LRU_HEAD_DIM = 128
CONV_WIDTH = 4
CONV_HIST = CONV_WIDTH - 1
LRU_C = 8.0
EVEN_IN_WIDTH = POOL_WIDTH + 2 * LRU_WIDTH
EVEN_MIX_WIDTH = POOL_WIDTH + LRU_WIDTH

MLA_HEADS = 8
QK_NOPE_DIM = 128
QK_ROPE_DIM = 64
ROPE_HALF = QK_ROPE_DIM // 2
V_HEAD_DIM = 128
Q_LORA_RANK = 384
KV_LORA_RANK = 256
ROPE_THETA = 10000.0
DOWN_WIDTH = Q_LORA_RANK + KV_LORA_RANK + LANES
QB_WIDTH = MLA_HEADS * QK_NOPE_DIM + (MLA_HEADS // 2) * LANES
QK_HEAD_WIDTH = 2 * LANES
QK_WIDTH = MLA_HEADS * QK_HEAD_WIDTH
V_WIDTH = MLA_HEADS * V_HEAD_DIM
ATTN_Q_ROWS = 512
ATTN_HEADS_PER_STEP = 4

D_FF = 4 * D_MODEL
FF_CHUNK = 512
DEEPNORM_ALPHA = (2 * DEPTH) ** 0.25
LN_EPS = 1e-5
RMS_EPS = 1e-6

MIX_FRAMES = 128
MIX_ROWS = MIX_FRAMES * BATCH
MIX_SUBTILES = 2
TOK_TILE = 1024
VMEM_LIMIT = 60 * 1024 * 1024

F32 = jnp.float32
BF16 = jnp.bfloat16
F32_TINY = float(np.finfo(np.float32).tiny)


def _resident(shape):
    zeros = (0,) * len(shape)
    return pl.BlockSpec(shape, lambda *_: zeros, pipeline_mode=pl.Buffered(1))


def _layer_norm(v, g, b):
    mu = jnp.mean(v, axis=-1, keepdims=True)
    d = v - mu
    var = jnp.mean(d * d, axis=-1, keepdims=True)
    return d * lax.rsqrt(var + LN_EPS) * g + b


def _rms_norm(v, g):
    return v * lax.rsqrt(jnp.mean(v * v, axis=-1, keepdims=True) + RMS_EPS) * g


def _even_mix_kernel(x_ref, w_in_ref, pool_w_ref, pool_scale_ref, conv_w_ref, conv_b_ref,
                     wa_ref, wx_ref, ba_ref, bx_ref, lam_ref, y_ref,
                     pool_ext, lru_ext, carry, h_buf, x_tm, y_tm):
    t = pl.program_id(0)
    PH = POOL_HIST * BATCH
    CH = CONV_HIST * BATCH

    @pl.when(t == 0)
    def _():
        pool_ext[0:PH, :] = jnp.zeros((PH, POOL_WIDTH), F32)
        lru_ext[0:CH, :] = jnp.zeros((CH, LRU_WIDTH), F32)
        carry[...] = jnp.zeros((BATCH, LRU_WIDTH), F32)

    h = carry[...]
    for s in range(MIX_SUBTILES):
        h = _even_mix_rows(s, t, h, x_ref, w_in_ref, pool_w_ref, pool_scale_ref, conv_w_ref,
                           conv_b_ref, wa_ref, wx_ref, ba_ref, bx_ref, lam_ref, y_ref,
                           pool_ext, lru_ext, h_buf, x_tm, y_tm)
    carry[...] = h
    pool_ext[0:PH, :] = pool_ext[MIX_ROWS:MIX_ROWS + PH, :]
    lru_ext[0:CH, :] = lru_ext[MIX_ROWS:MIX_ROWS + CH, :]


def _even_mix_rows(s, t, h, x_ref, w_in_ref, pool_w_ref, pool_scale_ref, conv_w_ref, conv_b_ref,
                   wa_ref, wx_ref, ba_ref, bx_ref, lam_ref, y_ref, pool_ext, lru_ext, h_buf,
                   x_tm, y_tm):
    R = MIX_ROWS // MIX_SUBTILES
    F = MIX_FRAMES // MIX_SUBTILES
    r0, f0 = s * R, s * F
    PH = POOL_HIST * BATCH
    CH = CONV_HIST * BATCH

    for b in range(BATCH):
        for cs in range(D_MODEL // LANES):
            x_tm[cs, pl.ds(r0 + b, F, stride=BATCH), :] = (
                x_ref[b, f0:f0 + F, cs * LANES:(cs + 1) * LANES])
    xb = jnp.concatenate([x_tm[cs, r0:r0 + R, :] for cs in range(D_MODEL // LANES)],
                         axis=-1).astype(BF16)
    proj = jnp.dot(xb, w_in_ref[...], preferred_element_type=F32)
    u_pool = proj[:, :POOL_WIDTH]
    u_lru = proj[:, POOL_WIDTH:POOL_WIDTH + LRU_WIDTH]
    u_gate = proj[:, POOL_WIDTH + LRU_WIDTH:]
    pool_ext[PH + r0:PH + r0 + R, :] = u_pool
    lru_ext[CH + r0:CH + r0 + R, :] = u_lru

    frame = (lax.broadcasted_iota(jnp.int32, (R, 1), 0) + r0) // BATCH + t * MIX_FRAMES
    for g, w in enumerate(POOL_WINDOWS):
        lo, hi = g * POOL_GROUP_WIDTH, (g + 1) * POOL_GROUP_WIDTH
        acc = pool_ext[r0:r0 + PH + R, lo:hi]
        k = 1
        while k < w:
            acc = acc[k * BATCH:, :] + acc[:acc.shape[0] - k * BATCH, :]
            k *= 2
        acc = acc[acc.shape[0] - R:, :]
        count = jnp.minimum(frame + 1, w).astype(F32)
        d = acc / count - u_pool[:, lo:hi]
        yg = jnp.dot(d.astype(BF16), pool_w_ref[g], preferred_element_type=F32)
        y_tm[g, r0:r0 + R, :] = yg * pool_scale_ref[:, lo:hi]

    cw = conv_w_ref[...]
    c = cw[CONV_WIDTH - 1:CONV_WIDTH, :] * u_lru + conv_b_ref[...]
    for k in range(CONV_WIDTH - 1):
        start = r0 + CH - (CONV_WIDTH - 1 - k) * BATCH
        c = c + cw[k:k + 1, :] * lru_ext[start:start + R, :]

    cb = c.astype(BF16)
    za, zx = [], []
    for p in range(LRU_HEADS // 2):
        cp = cb[:, p * 2 * LRU_HEAD_DIM:(p + 1) * 2 * LRU_HEAD_DIM]
        za.append(jnp.dot(cp, wa_ref[p], preferred_element_type=F32))
        zx.append(jnp.dot(cp, wx_ref[p], preferred_element_type=F32))
    ta = jnp.tanh(jnp.concatenate(za, axis=-1) + ba_ref[...])
    tx = jnp.tanh(jnp.concatenate(zx, axis=-1) + bx_ref[...])

    z = -lam_ref[...]
    softplus = jnp.maximum(z, 0.0) + jnp.log1p(jnp.exp(-jnp.abs(z)))
    half_rate = (-0.5 * LRU_C) * softplus
    log_a = ta * half_rate + half_rate
    a = jnp.exp(log_a)
    m2 = jnp.tanh(-log_a) * (a * a + 1.0)
    mult = m2 * lax.rsqrt(jnp.maximum(m2, F32_TINY))
    half_c = 0.5 * c
    xin = mult * (tx * half_c + half_c)

    for j in range(R // BATCH):
        rows = slice(j * BATCH, (j + 1) * BATCH)
        h = a[rows, :] * h + xin[rows, :]
        h_buf[r0 + j * BATCH:r0 + (j + 1) * BATCH, :] = h

    y_lru = h_buf[r0:r0 + R, :] * jax.nn.gelu(u_gate)
    n_pool = len(POOL_WINDOWS)
    for cs in range(LRU_WIDTH // LANES):
        y_tm[n_pool + cs, r0:r0 + R, :] = y_lru[:, cs * LANES:(cs + 1) * LANES]
    for b in range(BATCH):
        for cs in range(EVEN_MIX_WIDTH // LANES):
            y_ref[b, f0:f0 + F, cs * LANES:(cs + 1) * LANES] = (
                y_tm[cs, pl.ds(r0 + b, F, stride=BATCH), :].astype(BF16))
    return h


def _even_mix(x, w_in, pool_w, pool_scale, conv_w, conv_b, wa, wx, ba, bx, lam):
    R = MIX_ROWS
    return pl.pallas_call(
        _even_mix_kernel,
        grid=(SEQ // MIX_FRAMES,),
        in_specs=[
            pl.BlockSpec((BATCH, MIX_FRAMES, D_MODEL), lambda t: (0, t, 0)),
            _resident((D_MODEL, EVEN_IN_WIDTH)),
            _resident((len(POOL_WINDOWS), POOL_GROUP_WIDTH, POOL_GROUP_WIDTH)),
            _resident((1, POOL_WIDTH)),
            _resident((CONV_WIDTH, LRU_WIDTH)),
            _resident((1, LRU_WIDTH)),
            _resident((LRU_HEADS // 2, 2 * LRU_HEAD_DIM, 2 * LRU_HEAD_DIM)),
            _resident((LRU_HEADS // 2, 2 * LRU_HEAD_DIM, 2 * LRU_HEAD_DIM)),
            _resident((1, LRU_WIDTH)),
            _resident((1, LRU_WIDTH)),
            _resident((1, LRU_WIDTH)),
        ],
        out_specs=pl.BlockSpec((BATCH, MIX_FRAMES, EVEN_MIX_WIDTH), lambda t: (0, t, 0)),
        out_shape=jax.ShapeDtypeStruct((BATCH, SEQ, EVEN_MIX_WIDTH), BF16),
        scratch_shapes=[
            pltpu.VMEM((R + POOL_HIST * BATCH, POOL_WIDTH), F32),
            pltpu.VMEM((R + CONV_HIST * BATCH, LRU_WIDTH), F32),
            pltpu.VMEM((BATCH, LRU_WIDTH), F32),
            pltpu.VMEM((R, LRU_WIDTH), F32),
            pltpu.VMEM((D_MODEL // LANES, R, LANES), F32),
            pltpu.VMEM((EVEN_MIX_WIDTH // LANES, R, LANES), F32),
        ],
        compiler_params=pltpu.CompilerParams(
            dimension_semantics=("arbitrary",),
            vmem_limit_bytes=VMEM_LIMIT),
        name="even_mix",
    )(x, w_in, pool_w, pool_scale, conv_w, conv_b, wa, wx, ba, bx, lam)


def _out_mlp_kernel(y_ref, x_ref, w_out_ref, g1_ref, b1_ref, w1_ref, w2_ref, g2_ref, b2_ref,
                    o_ref):
    mix = jnp.dot(y_ref[...], w_out_ref[...], preferred_element_type=F32)
    x1 = _layer_norm(DEEPNORM_ALPHA * x_ref[...] + mix, g1_ref[...], b1_ref[...])
    xb = x1.astype(BF16)
    acc = None
    for c in range(D_FF // FF_CHUNK):
        lo, hi = c * FF_CHUNK, (c + 1) * FF_CHUNK
        h = jnp.dot(xb, w1_ref[:, lo:hi], preferred_element_type=F32)
        h = jnp.square(jnp.maximum(h, 0.0)).astype(BF16)
        part = jnp.dot(h, w2_ref[lo:hi, :], preferred_element_type=F32)
        acc = part if acc is None else acc + part
    o_ref[...] = _layer_norm(DEEPNORM_ALPHA * x1 + acc, g2_ref[...], b2_ref[...])


def _out_mlp(y, x, w_out, g1, b1, w1, w2, g2, b2):
    T = TOK_TILE
    kd = y.shape[-1]
    return pl.pallas_call(
        _out_mlp_kernel,
        grid=(N_TOK // T,),
        in_specs=[
            pl.BlockSpec((T, kd), lambda i: (i, 0)),
            pl.BlockSpec((T, D_MODEL), lambda i: (i, 0)),
            _resident((kd, D_MODEL)),
            _resident((1, D_MODEL)),
            _resident((1, D_MODEL)),
            _resident((D_MODEL, D_FF)),
            _resident((D_FF, D_MODEL)),
            _resident((1, D_MODEL)),
            _resident((1, D_MODEL)),
        ],
        out_specs=pl.BlockSpec((T, D_MODEL), lambda i: (i, 0)),
        out_shape=jax.ShapeDtypeStruct((N_TOK, D_MODEL), F32),
        compiler_params=pltpu.CompilerParams(
            dimension_semantics=("arbitrary",),
            vmem_limit_bytes=VMEM_LIMIT),
        name="out_mlp",
    )(y, x, w_out, g1, b1, w1, w2, g2, b2)


def _mla_proj_kernel(x_ref, pos_ref, invf_ref, w_down_ref, qg_ref, kvg_ref, w_qb_ref, w_kvb_ref,
                     q_ref, k_ref, v_ref):
    xb = x_ref[...].astype(BF16)
    down = jnp.dot(xb, w_down_ref[...], preferred_element_type=F32)
    cq = _rms_norm(down[:, :Q_LORA_RANK], qg_ref[...])
    ckv = _rms_norm(down[:, Q_LORA_RANK:Q_LORA_RANK + KV_LORA_RANK], kvg_ref[...])
    kpe = down[:, Q_LORA_RANK + KV_LORA_RANK:]

    ang = pos_ref[...].astype(F32) * invf_ref[...]
    cos = jnp.cos(ang)
    lane = lax.broadcasted_iota(jnp.int32, (1, LANES), 1)
    sin = jnp.sin(ang) * jnp.where(lane < 2 * ROPE_HALF, -1.0, 1.0)

    def rope(blk):
        return blk * cos + pltpu.roll(blk, 2 * ROPE_HALF, axis=1) * sin

    q = jnp.dot(cq.astype(BF16), w_qb_ref[...], preferred_element_type=F32)
    n_nope = MLA_HEADS * QK_NOPE_DIM
    q_pe = [rope(q[:, n_nope + p * LANES:n_nope + (p + 1) * LANES]).astype(BF16)
            for p in range(MLA_HEADS // 2)]
    kv = jnp.dot(ckv.astype(BF16), w_kvb_ref[...], preferred_element_type=F32)
    kr = rope(kpe)
    first_slot = (lane % (2 * ROPE_HALF)) < ROPE_HALF
    k_pe = [jnp.where(first_slot, kr, 0.0).astype(BF16),
            jnp.where(first_slot, 0.0, kr).astype(BF16)]
    for h in range(MLA_HEADS):
        lo = h * QK_HEAD_WIDTH
        q_ref[:, lo:lo + LANES] = q[:, h * LANES:(h + 1) * LANES].astype(BF16)
        q_ref[:, lo + LANES:lo + 2 * LANES] = q_pe[h // 2]
        k_ref[:, lo:lo + LANES] = kv[:, h * LANES:(h + 1) * LANES].astype(BF16)
        k_ref[:, lo + LANES:lo + 2 * LANES] = k_pe[h % 2]
    v_ref[...] = kv[:, n_nope:].astype(BF16)


def _mla_proj(x, pos, invf, w_down, qg, kvg, w_qb, w_kvb):
    T = TOK_TILE
    return pl.pallas_call(
        _mla_proj_kernel,
        grid=(N_TOK // T,),
        in_specs=[
            pl.BlockSpec((T, D_MODEL), lambda i: (i, 0)),
            pl.BlockSpec((T, 1), lambda i: (i, 0)),
            _resident((1, LANES)),
            _resident((D_MODEL, DOWN_WIDTH)),
            _resident((1, Q_LORA_RANK)),
            _resident((1, KV_LORA_RANK)),
            _resident((Q_LORA_RANK, QB_WIDTH)),
            _resident((KV_LORA_RANK, 2 * V_WIDTH)),
        ],
        out_specs=[
            pl.BlockSpec((T, QK_WIDTH), lambda i: (i, 0)),
            pl.BlockSpec((T, QK_WIDTH), lambda i: (i, 0)),
            pl.BlockSpec((T, V_WIDTH), lambda i: (i, 0)),
        ],
        out_shape=[
            jax.ShapeDtypeStruct((N_TOK, QK_WIDTH), BF16),
            jax.ShapeDtypeStruct((N_TOK, QK_WIDTH), BF16),
            jax.ShapeDtypeStruct((N_TOK, V_WIDTH), BF16),
        ],
        compiler_params=pltpu.CompilerParams(
            dimension_semantics=("arbitrary",),
            vmem_limit_bytes=VMEM_LIMIT),
        name="mla_proj",
    )(x, pos, invf, w_down, qg, kvg, w_qb, w_kvb)


_NT = (((1,), (1,)), ((), ()))


def _mla_attn_kernel(q_ref, k_ref, v_ref, o_ref, v_ones):
    c = (QK_NOPE_DIM + QK_ROPE_DIM) ** -0.5 * float(np.log2(np.e))
    neg = jnp.finfo(F32).min
    R = ATTN_Q_ROWS
    row = lax.broadcasted_iota(jnp.int32, (R, R), 0)
    col = lax.broadcasted_iota(jnp.int32, (R, R), 1)
    hidden = (col // CHUNK) > (row // CHUNK)
    for hh in range(ATTN_HEADS_PER_STEP):
        v_ones[hh, :, :V_HEAD_DIM] = v_ref[:, hh * V_HEAD_DIM:(hh + 1) * V_HEAD_DIM]
        v_ones[hh, :, V_HEAD_DIM:] = jnp.ones((SEQ, LANES), BF16)
    for j in range(SEQ // R):
        qs, ke = j * R, (j + 1) * R
        for hh in range(ATTN_HEADS_PER_STEP):
            lo, hi = hh * QK_HEAD_WIDTH, (hh + 1) * QK_HEAD_WIDTH
            s = lax.dot_general(q_ref[qs:ke, lo:hi], k_ref[:ke, lo:hi], _NT,
                                preferred_element_type=F32) * c
            diag = jnp.where(hidden, neg, s[:, qs:ke])
            s = diag if j == 0 else jnp.concatenate([s[:, :qs], diag], axis=-1)
            m = jnp.max(s, axis=-1, keepdims=True)
            p = jnp.exp2(s - m).astype(BF16)
            ol = jnp.dot(p, v_ones[hh, :ke, :], preferred_element_type=F32)
            o = ol[:, :V_HEAD_DIM] / ol[:, V_HEAD_DIM:V_HEAD_DIM + 1]
            o_ref[qs:ke, hh * V_HEAD_DIM:(hh + 1) * V_HEAD_DIM] = o.astype(BF16)


def _mla_attn(q, k, v):
    hps = ATTN_HEADS_PER_STEP
    blk = lambda b, g: (b, 0, g)
    return pl.pallas_call(
        _mla_attn_kernel,
        grid=(BATCH, MLA_HEADS // hps),
        in_specs=[
            pl.BlockSpec((None, SEQ, hps * QK_HEAD_WIDTH), blk),
            pl.BlockSpec((None, SEQ, hps * QK_HEAD_WIDTH), blk),
            pl.BlockSpec((None, SEQ, hps * V_HEAD_DIM), blk),
        ],
        out_specs=pl.BlockSpec((None, SEQ, hps * V_HEAD_DIM), blk),
        out_shape=jax.ShapeDtypeStruct((BATCH, SEQ, V_WIDTH), BF16),
        scratch_shapes=[pltpu.VMEM((hps, SEQ, V_HEAD_DIM + LANES), BF16)],
        compiler_params=pltpu.CompilerParams(
            dimension_semantics=("arbitrary", "arbitrary"),
            vmem_limit_bytes=VMEM_LIMIT),
        name="mla_attn",
    )(q, k, v)


def _q_weight(w_qb):
    w = w_qb.reshape(Q_LORA_RANK, MLA_HEADS, QK_NOPE_DIM + QK_ROPE_DIM)
    nope = w[:, :, :QK_NOPE_DIM].reshape(Q_LORA_RANK, MLA_HEADS * QK_NOPE_DIM)
    pe = w[:, :, QK_NOPE_DIM:].reshape(Q_LORA_RANK, MLA_HEADS // 2, 2, 2, ROPE_HALF)
    pe = pe.transpose(0, 1, 3, 2, 4).reshape(Q_LORA_RANK, (MLA_HEADS // 2) * LANES)
    return jnp.concatenate([nope, pe], axis=-1).astype(BF16)


def _kv_weight(w_kvb):
    w = w_kvb.reshape(KV_LORA_RANK, MLA_HEADS, 2, QK_NOPE_DIM)
    return w.transpose(0, 2, 1, 3).reshape(KV_LORA_RANK, 2 * V_WIDTH).astype(BF16)


def _down_weight(w_down):
    base = Q_LORA_RANK + KV_LORA_RANK
    x1 = w_down[:, base:base + ROPE_HALF]
    x2 = w_down[:, base + ROPE_HALF:]
    return jnp.concatenate([w_down[:, :base], x1, x1, x2, x2], axis=-1).astype(BF16)


def _pair_block_diag(w):
    w = w.reshape(LRU_HEADS // 2, 2, LRU_HEAD_DIM, LRU_HEAD_DIM)
    z = jnp.zeros_like(w[:, 0])
    top = jnp.concatenate([w[:, 0], z], axis=-1)
    bot = jnp.concatenate([z, w[:, 1]], axis=-1)
    return jnp.concatenate([top, bot], axis=-2)


def kernel(x, positions, ln_mix_g, ln_mix_b, ln_ffn_g, ln_ffn_b, even_w_in, pool_w, pool_scale, lru_conv_w, lru_conv_b, lru_w_a, lru_b_a, lru_w_x, lru_b_x, lru_lambda, even_w_out, mla_w_down, mla_q_norm_g, mla_kv_norm_g, mla_w_qb, mla_w_kvb, mla_w_o, mlp_w1, mlp_w2):
    row = lambda v: v.reshape(1, -1)
    inv_freq = ROPE_THETA ** (-jnp.arange(0, QK_ROPE_DIM, 2, dtype=F32) / QK_ROPE_DIM)
    invf = jnp.tile(inv_freq, LANES // ROPE_HALF).reshape(1, LANES)
    pos = positions.reshape(N_TOK, 1)

    xf = x.reshape(N_TOK, D_MODEL)
    for layer in range(DEPTH):
        j = layer // 2
        if layer % 2 == 0:
            y = _even_mix(
                xf.reshape(BATCH, SEQ, D_MODEL), even_w_in[j].astype(BF16), pool_w[j].astype(BF16),
                row(pool_scale[j]), lru_conv_w[j], row(lru_conv_b[j]),
                _pair_block_diag(0.5 * lru_w_a[j]).astype(BF16),
                _pair_block_diag(0.5 * lru_w_x[j]).astype(BF16),
                row(0.5 * lru_b_a[j]), row(0.5 * lru_b_x[j]), row(lru_lambda[j]))
            y = y.reshape(N_TOK, EVEN_MIX_WIDTH)
            w_out = even_w_out[j]
        else:
            q, k, v = _mla_proj(
                xf, pos, invf, _down_weight(mla_w_down[j]),
                row(mla_q_norm_g[j]), row(mla_kv_norm_g[j]),
                _q_weight(mla_w_qb[j]), _kv_weight(mla_w_kvb[j]))
            y = _mla_attn(q.reshape(BATCH, SEQ, QK_WIDTH), k.reshape(BATCH, SEQ, QK_WIDTH),
                          v.reshape(BATCH, SEQ, V_WIDTH)).reshape(N_TOK, V_WIDTH)
            w_out = mla_w_o[j]
        xf = _out_mlp(y, xf, w_out.astype(BF16), row(ln_mix_g[layer]), row(ln_mix_b[layer]),
                      mlp_w1[layer].astype(BF16), mlp_w2[layer].astype(BF16),
                      row(ln_ffn_g[layer]), row(ln_ffn_b[layer]))
    return xf.reshape(BATCH, SEQ, D_MODEL)
```

```python
import functools

import numpy as np
import jax
import jax.numpy as jnp
from jax import lax
from jax.experimental import pallas as pl
from jax.experimental.pallas import tpu as pltpu

D_MODEL = 1024
BATCH = 8
SEQ = 2048
DEPTH = 4
N_TOK = BATCH * SEQ

CHUNK = 64

LANES = 128
SUBLANES = 8
assert BATCH == SUBLANES

POOL_WINDOWS = (2, 4, 8, 16)
POOL_GROUP_WIDTH = 128
POOL_WIDTH = 512
POOL_HIST = 16
LRU_WIDTH = 1024
LRU_HEADS = 8
LRU_HEAD_DIM = 128
CONV_WIDTH = 4
CONV_HIST = CONV_WIDTH - 1
LRU_C = 8.0
EVEN_IN_WIDTH = POOL_WIDTH + 2 * LRU_WIDTH
EVEN_MIX_WIDTH = POOL_WIDTH + LRU_WIDTH

MLA_HEADS = 8
QK_NOPE_DIM = 128
QK_ROPE_DIM = 64
ROPE_HALF = QK_ROPE_DIM // 2
V_HEAD_DIM = 128
Q_LORA_RANK = 384
KV_LORA_RANK = 256
ROPE_THETA = 10000.0
DOWN_WIDTH = Q_LORA_RANK + KV_LORA_RANK + LANES
QB_WIDTH = MLA_HEADS * QK_NOPE_DIM + (MLA_HEADS // 2) * LANES
QK_HEAD_WIDTH = 2 * LANES
QK_WIDTH = MLA_HEADS * QK_HEAD_WIDTH
V_WIDTH = MLA_HEADS * V_HEAD_DIM
ATTN_Q_ROWS = 512
ATTN_HEADS_PER_STEP = 4

D_FF = 4 * D_MODEL
FF_CHUNK = 512
DEEPNORM_ALPHA = (2 * DEPTH) ** 0.25
LN_EPS = 1e-5
RMS_EPS = 1e-6

MIX_FRAMES = 128
MIX_ROWS = MIX_FRAMES * BATCH
MIX_SUBTILES = 2
TOK_TILE = 1024
STAGE_ROWS = 512
VMEM_LIMIT = 60 * 1024 * 1024

F32 = jnp.float32
BF16 = jnp.bfloat16
F32_TINY = float(np.finfo(np.float32).tiny)


def _resident(shape):
    zeros = (0,) * len(shape)
    return pl.BlockSpec(shape, lambda *_: zeros, pipeline_mode=pl.Buffered(1))


def _layer_norm(v, g, b):
    mu = jnp.mean(v, axis=-1, keepdims=True)
    d = v - mu
    var = jnp.mean(d * d, axis=-1, keepdims=True)
    return d * lax.rsqrt(var + LN_EPS) * g + b


def _rms_norm(v, g):
    return v * lax.rsqrt(jnp.mean(v * v, axis=-1, keepdims=True) + RMS_EPS) * g


def _even_mix_kernel(x_ref, w_in_ref, pool_w_ref, pool_scale_ref, conv_w_ref, conv_b_ref,
                     wa_ref, wx_ref, ba_ref, bx_ref, lam_ref, y_ref,
                     pool_ext, lru_ext, carry, h_buf, x_tm, y_tm):
    t = pl.program_id(0)
    PH = POOL_HIST * BATCH
    CH = CONV_HIST * BATCH

    @pl.when(t == 0)
    def _():
        pool_ext[0:PH, :] = jnp.zeros((PH, POOL_WIDTH), F32)
        lru_ext[0:CH, :] = jnp.zeros((CH, LRU_WIDTH), F32)
        carry[...] = jnp.zeros((BATCH, LRU_WIDTH), F32)

    h = carry[...]
    for s in range(MIX_SUBTILES):
        h = _even_mix_rows(s, t, h, x_ref, w_in_ref, pool_w_ref, pool_scale_ref, conv_w_ref,
                           conv_b_ref, wa_ref, wx_ref, ba_ref, bx_ref, lam_ref, y_ref,
                           pool_ext, lru_ext, h_buf, x_tm, y_tm)
    carry[...] = h
    pool_ext[0:PH, :] = pool_ext[MIX_ROWS:MIX_ROWS + PH, :]
    lru_ext[0:CH, :] = lru_ext[MIX_ROWS:MIX_ROWS + CH, :]


def _even_mix_rows(s, t, h, x_ref, w_in_ref, pool_w_ref, pool_scale_ref, conv_w_ref, conv_b_ref,
                   wa_ref, wx_ref, ba_ref, bx_ref, lam_ref, y_ref, pool_ext, lru_ext, h_buf,
                   x_tm, y_tm):
    R = MIX_ROWS // MIX_SUBTILES
    F = MIX_FRAMES // MIX_SUBTILES
    r0, f0 = s * R, s * F
    PH = POOL_HIST * BATCH
    CH = CONV_HIST * BATCH

    for b in range(BATCH):
        for cs in range(D_MODEL // LANES):
            x_tm[cs, pl.ds(r0 + b, F, stride=BATCH), :] = (
                x_ref[b, f0:f0 + F, cs * LANES:(cs + 1) * LANES])
    xb = jnp.concatenate([x_tm[cs, r0:r0 + R, :] for cs in range(D_MODEL // LANES)],
                         axis=-1).astype(BF16)
    proj = jnp.dot(xb, w_in_ref[...], preferred_element_type=F32)
    u_pool = proj[:, :POOL_WIDTH]
    u_lru = proj[:, POOL_WIDTH:POOL_WIDTH + LRU_WIDTH]
    u_gate = proj[:, POOL_WIDTH + LRU_WIDTH:]
    pool_ext[PH + r0:PH + r0 + R, :] = u_pool
    lru_ext[CH + r0:CH + r0 + R, :] = u_lru

    frame = (lax.broadcasted_iota(jnp.int32, (R, 1), 0) + r0) // BATCH + t * MIX_FRAMES
    for g, w in enumerate(POOL_WINDOWS):
        lo, hi = g * POOL_GROUP_WIDTH, (g + 1) * POOL_GROUP_WIDTH
        acc = pool_ext[r0:r0 + PH + R, lo:hi]
        k = 1
        while k < w:
            acc = acc[k * BATCH:, :] + acc[:acc.shape[0] - k * BATCH, :]
            k *= 2
        acc = acc[acc.shape[0] - R:, :]
        count = jnp.minimum(frame + 1, w).astype(F32)
        d = acc / count - u_pool[:, lo:hi]
        yg = jnp.dot(d.astype(BF16), pool_w_ref[g], preferred_element_type=F32)
        y_tm[g, r0:r0 + R, :] = yg * pool_scale_ref[:, lo:hi]

    cw = conv_w_ref[...]
    c = cw[CONV_WIDTH - 1:CONV_WIDTH, :] * u_lru + conv_b_ref[...]
    for k in range(CONV_WIDTH - 1):
        start = r0 + CH - (CONV_WIDTH - 1 - k) * BATCH
        c = c + cw[k:k + 1, :] * lru_ext[start:start + R, :]

    cb = c.astype(BF16)
    za, zx = [], []
    for p in range(LRU_HEADS // 2):
        cp = cb[:, p * 2 * LRU_HEAD_DIM:(p + 1) * 2 * LRU_HEAD_DIM]
        za.append(jnp.dot(cp, wa_ref[p], preferred_element_type=F32))
        zx.append(jnp.dot(cp, wx_ref[p], preferred_element_type=F32))
    ta = jnp.tanh(jnp.concatenate(za, axis=-1) + ba_ref[...])
    tx = jnp.tanh(jnp.concatenate(zx, axis=-1) + bx_ref[...])

    z = -lam_ref[...]
    softplus = jnp.maximum(z, 0.0) + jnp.log1p(jnp.exp(-jnp.abs(z)))
    half_rate = (-0.5 * LRU_C) * softplus
    log_a = ta * half_rate + half_rate
    a = jnp.exp(log_a)
    m2 = jnp.tanh(-log_a) * (a * a + 1.0)
    mult = m2 * lax.rsqrt(jnp.maximum(m2, F32_TINY))
    half_c = 0.5 * c
    xin = mult * (tx * half_c + half_c)

    for j in range(R // BATCH):
        rows = slice(j * BATCH, (j + 1) * BATCH)
        h = a[rows, :] * h + xin[rows, :]
        h_buf[r0 + j * BATCH:r0 + (j + 1) * BATCH, :] = h

    y_lru = h_buf[r0:r0 + R, :] * jax.nn.gelu(u_gate)
    n_pool = len(POOL_WINDOWS)
    for cs in range(LRU_WIDTH // LANES):
        y_tm[n_pool + cs, r0:r0 + R, :] = y_lru[:, cs * LANES:(cs + 1) * LANES]
    for b in range(BATCH):
        for cs in range(EVEN_MIX_WIDTH // LANES):
            y_ref[b, f0:f0 + F, cs * LANES:(cs + 1) * LANES] = (
                y_tm[cs, pl.ds(r0 + b, F, stride=BATCH), :].astype(BF16))
    return h


def _even_mix(x, w_in, pool_w, pool_scale, conv_w, conv_b, wa, wx, ba, bx, lam):
    R = MIX_ROWS
    return pl.pallas_call(
        _even_mix_kernel,
        grid=(SEQ // MIX_FRAMES,),
        in_specs=[
            pl.BlockSpec((BATCH, MIX_FRAMES, D_MODEL), lambda t: (0, t, 0)),
            _resident((D_MODEL, EVEN_IN_WIDTH)),
            _resident((len(POOL_WINDOWS), POOL_GROUP_WIDTH, POOL_GROUP_WIDTH)),
            _resident((1, POOL_WIDTH)),
            _resident((CONV_WIDTH, LRU_WIDTH)),
            _resident((1, LRU_WIDTH)),
            _resident((LRU_HEADS // 2, 2 * LRU_HEAD_DIM, 2 * LRU_HEAD_DIM)),
            _resident((LRU_HEADS // 2, 2 * LRU_HEAD_DIM, 2 * LRU_HEAD_DIM)),
            _resident((1, LRU_WIDTH)),
            _resident((1, LRU_WIDTH)),
            _resident((1, LRU_WIDTH)),
        ],
        out_specs=pl.BlockSpec((BATCH, MIX_FRAMES, EVEN_MIX_WIDTH), lambda t: (0, t, 0)),
        out_shape=jax.ShapeDtypeStruct((BATCH, SEQ, EVEN_MIX_WIDTH), BF16),
        scratch_shapes=[
            pltpu.VMEM((R + POOL_HIST * BATCH, POOL_WIDTH), F32),
            pltpu.VMEM((R + CONV_HIST * BATCH, LRU_WIDTH), F32),
            pltpu.VMEM((BATCH, LRU_WIDTH), F32),
            pltpu.VMEM((R, LRU_WIDTH), F32),
            pltpu.VMEM((D_MODEL // LANES, R, LANES), F32),
            pltpu.VMEM((EVEN_MIX_WIDTH // LANES, R, LANES), F32),
        ],
        compiler_params=pltpu.CompilerParams(
            dimension_semantics=("arbitrary",),
            vmem_limit_bytes=VMEM_LIMIT),
        name="even_mix",
    )(x, w_in, pool_w, pool_scale, conv_w, conv_b, wa, wx, ba, bx, lam)


def _weight_chunks(src, dst):
    rows, cols = dst.shape
    return [(src.at[r:r + STAGE_ROWS, c:c + D_MODEL], dst.at[r:r + STAGE_ROWS, c:c + D_MODEL])
            for r in range(0, rows, STAGE_ROWS) for c in range(0, cols, D_MODEL)]


def _load_weights_as_bf16(chunks, stage, sem):
    copy = lambda i: pltpu.make_async_copy(chunks[i][0], stage.at[i % 2], sem.at[i % 2])
    copy(0).start()
    for i in range(len(chunks)):
        if i + 1 < len(chunks):
            copy(i + 1).start()
        copy(i).wait()
        chunks[i][1][...] = stage[i % 2].astype(BF16)


def _out_mlp_kernel(layer, y_ref, x_ref, w_out_hbm, g1_ref, b1_ref, w1_hbm, w2_hbm, g2_ref, b2_ref,
                    o_ref, w_out_ref, w1_ref, w2_ref, stage, sem):
    @pl.when(pl.program_id(0) == 0)
    def _():
        _load_weights_as_bf16(
            _weight_chunks(w_out_hbm.at[layer // 2], w_out_ref)
            + _weight_chunks(w1_hbm.at[layer], w1_ref)
            + _weight_chunks(w2_hbm.at[layer], w2_ref), stage, sem)

    mix = jnp.dot(y_ref[...], w_out_ref[...], preferred_element_type=F32)
    x1 = _layer_norm(DEEPNORM_ALPHA * x_ref[...] + mix, g1_ref[...], b1_ref[...])
    xb = x1.astype(BF16)
    acc = None
    for c in range(D_FF // FF_CHUNK):
        lo, hi = c * FF_CHUNK, (c + 1) * FF_CHUNK
        h = jnp.dot(xb, w1_ref[:, lo:hi], preferred_element_type=F32)
        h = jnp.square(jnp.maximum(h, 0.0)).astype(BF16)
        part = jnp.dot(h, w2_ref[lo:hi, :], preferred_element_type=F32)
        acc = part if acc is None else acc + part
    o_ref[...] = _layer_norm(DEEPNORM_ALPHA * x1 + acc, g2_ref[...], b2_ref[...])


def _out_mlp(layer, y, x, w_out_all, g1, b1, w1_all, w2_all, g2, b2):
    T = TOK_TILE
    kd = y.shape[-1]
    hbm = pl.BlockSpec(memory_space=pl.ANY)
    return pl.pallas_call(
        functools.partial(_out_mlp_kernel, layer),
        grid=(N_TOK // T,),
        in_specs=[
            pl.BlockSpec((T, kd), lambda i: (i, 0)),
            pl.BlockSpec((T, D_MODEL), lambda i: (i, 0)),
            hbm,
            _resident((1, D_MODEL)),
            _resident((1, D_MODEL)),
            hbm,
            hbm,
            _resident((1, D_MODEL)),
            _resident((1, D_MODEL)),
        ],
        out_specs=pl.BlockSpec((T, D_MODEL), lambda i: (i, 0)),
        out_shape=jax.ShapeDtypeStruct((N_TOK, D_MODEL), F32),
        scratch_shapes=[
            pltpu.VMEM((kd, D_MODEL), BF16),
            pltpu.VMEM((D_MODEL, D_FF), BF16),
            pltpu.VMEM((D_FF, D_MODEL), BF16),
            pltpu.VMEM((2, STAGE_ROWS, D_MODEL), F32),
            pltpu.SemaphoreType.DMA((2,)),
        ],
        compiler_params=pltpu.CompilerParams(
            dimension_semantics=("arbitrary",),
            vmem_limit_bytes=VMEM_LIMIT),
        name="out_mlp",
    )(y, x, w_out_all, g1, b1, w1_all, w2_all, g2, b2)


def _mla_proj_kernel(x_ref, pos_ref, invf_ref, w_down_ref, qg_ref, kvg_ref, w_qb_ref, w_kvb_ref,
                     q_ref, k_ref, v_ref):
    xb = x_ref[...].astype(BF16)
    down = jnp.dot(xb, w_down_ref[...], preferred_element_type=F32)
    cq = _rms_norm(down[:, :Q_LORA_RANK], qg_ref[...])
    ckv = _rms_norm(down[:, Q_LORA_RANK:Q_LORA_RANK + KV_LORA_RANK], kvg_ref[...])
    kpe = down[:, Q_LORA_RANK + KV_LORA_RANK:]

    ang = pos_ref[...].astype(F32) * invf_ref[...]
    cos = jnp.cos(ang)
    lane = lax.broadcasted_iota(jnp.int32, (1, LANES), 1)
    sin = jnp.sin(ang) * jnp.where(lane < 2 * ROPE_HALF, -1.0, 1.0)

    def rope(blk):
        return blk * cos + pltpu.roll(blk, 2 * ROPE_HALF, axis=1) * sin

    q = jnp.dot(cq.astype(BF16), w_qb_ref[...], preferred_element_type=F32)
    n_nope = MLA_HEADS * QK_NOPE_DIM
    q_pe = [rope(q[:, n_nope + p * LANES:n_nope + (p + 1) * LANES]).astype(BF16)
            for p in range(MLA_HEADS // 2)]
    kv = jnp.dot(ckv.astype(BF16), w_kvb_ref[...], preferred_element_type=F32)
    kr = rope(kpe)
    first_slot = (lane % (2 * ROPE_HALF)) < ROPE_HALF
    k_pe = [jnp.where(first_slot, kr, 0.0).astype(BF16),
            jnp.where(first_slot, 0.0, kr).astype(BF16)]
    for h in range(MLA_HEADS):
        lo = h * QK_HEAD_WIDTH
        q_ref[:, lo:lo + LANES] = q[:, h * LANES:(h + 1) * LANES].astype(BF16)
        q_ref[:, lo + LANES:lo + 2 * LANES] = q_pe[h // 2]
        k_ref[:, lo:lo + LANES] = kv[:, h * LANES:(h + 1) * LANES].astype(BF16)
        k_ref[:, lo + LANES:lo + 2 * LANES] = k_pe[h % 2]
    v_ref[...] = kv[:, n_nope:].astype(BF16)


def _mla_proj(x, pos, invf, w_down, qg, kvg, w_qb, w_kvb):
    T = TOK_TILE
    return pl.pallas_call(
        _mla_proj_kernel,
        grid=(N_TOK // T,),
        in_specs=[
            pl.BlockSpec((T, D_MODEL), lambda i: (i, 0)),
            pl.BlockSpec((T, 1), lambda i: (i, 0)),
            _resident((1, LANES)),
            _resident((D_MODEL, DOWN_WIDTH)),
            _resident((1, Q_LORA_RANK)),
            _resident((1, KV_LORA_RANK)),
            _resident((Q_LORA_RANK, QB_WIDTH)),
            _resident((KV_LORA_RANK, 2 * V_WIDTH)),
        ],
        out_specs=[
            pl.BlockSpec((T, QK_WIDTH), lambda i: (i, 0)),
            pl.BlockSpec((T, QK_WIDTH), lambda i: (i, 0)),
            pl.BlockSpec((T, V_WIDTH), lambda i: (i, 0)),
        ],
        out_shape=[
            jax.ShapeDtypeStruct((N_TOK, QK_WIDTH), BF16),
            jax.ShapeDtypeStruct((N_TOK, QK_WIDTH), BF16),
            jax.ShapeDtypeStruct((N_TOK, V_WIDTH), BF16),
        ],
        compiler_params=pltpu.CompilerParams(
            dimension_semantics=("arbitrary",),
            vmem_limit_bytes=VMEM_LIMIT),
        name="mla_proj",
    )(x, pos, invf, w_down, qg, kvg, w_qb, w_kvb)


_NT = (((1,), (1,)), ((), ()))


def _mla_attn_kernel(q_ref, k_ref, v_ref, o_ref, v_ones):
    c = (QK_NOPE_DIM + QK_ROPE_DIM) ** -0.5 * float(np.log2(np.e))
    neg = jnp.finfo(F32).min
    R = ATTN_Q_ROWS
    row = lax.broadcasted_iota(jnp.int32, (R, R), 0)
    col = lax.broadcasted_iota(jnp.int32, (R, R), 1)
    hidden = (col // CHUNK) > (row // CHUNK)
    for hh in range(ATTN_HEADS_PER_STEP):
        v_ones[hh, :, :V_HEAD_DIM] = v_ref[:, hh * V_HEAD_DIM:(hh + 1) * V_HEAD_DIM]
        v_ones[hh, :, V_HEAD_DIM:] = jnp.ones((SEQ, LANES), BF16)
    for j in range(SEQ // R):
        qs, ke = j * R, (j + 1) * R
        for hh in range(ATTN_HEADS_PER_STEP):
            lo, hi = hh * QK_HEAD_WIDTH, (hh + 1) * QK_HEAD_WIDTH
            s = lax.dot_general(q_ref[qs:ke, lo:hi], k_ref[:ke, lo:hi], _NT,
                                preferred_element_type=F32) * c
            diag = jnp.where(hidden, neg, s[:, qs:ke])
            s = diag if j == 0 else jnp.concatenate([s[:, :qs], diag], axis=-1)
            m = jnp.max(s, axis=-1, keepdims=True)
            p = jnp.exp2(s - m).astype(BF16)
            ol = jnp.dot(p, v_ones[hh, :ke, :], preferred_element_type=F32)
            o = ol[:, :V_HEAD_DIM] / ol[:, V_HEAD_DIM:V_HEAD_DIM + 1]
            o_ref[qs:ke, hh * V_HEAD_DIM:(hh + 1) * V_HEAD_DIM] = o.astype(BF16)


def _mla_attn(q, k, v):
    hps = ATTN_HEADS_PER_STEP
    blk = lambda b, g: (b, 0, g)
    return pl.pallas_call(
        _mla_attn_kernel,
        grid=(BATCH, MLA_HEADS // hps),
        in_specs=[
            pl.BlockSpec((None, SEQ, hps * QK_HEAD_WIDTH), blk),
            pl.BlockSpec((None, SEQ, hps * QK_HEAD_WIDTH), blk),
            pl.BlockSpec((None, SEQ, hps * V_HEAD_DIM), blk),
        ],
        out_specs=pl.BlockSpec((None, SEQ, hps * V_HEAD_DIM), blk),
        out_shape=jax.ShapeDtypeStruct((BATCH, SEQ, V_WIDTH), BF16),
        scratch_shapes=[pltpu.VMEM((hps, SEQ, V_HEAD_DIM + LANES), BF16)],
        compiler_params=pltpu.CompilerParams(
            dimension_semantics=("arbitrary", "arbitrary"),
            vmem_limit_bytes=VMEM_LIMIT),
        name="mla_attn",
    )(q, k, v)


def _q_weight(w_qb):
    w = w_qb.reshape(Q_LORA_RANK, MLA_HEADS, QK_NOPE_DIM + QK_ROPE_DIM)
    nope = w[:, :, :QK_NOPE_DIM].reshape(Q_LORA_RANK, MLA_HEADS * QK_NOPE_DIM)
    pe = w[:, :, QK_NOPE_DIM:].reshape(Q_LORA_RANK, MLA_HEADS // 2, 2, 2, ROPE_HALF)
    pe = pe.transpose(0, 1, 3, 2, 4).reshape(Q_LORA_RANK, (MLA_HEADS // 2) * LANES)
    return jnp.concatenate([nope, pe], axis=-1).astype(BF16)


def _kv_weight(w_kvb):
    w = w_kvb.reshape(KV_LORA_RANK, MLA_HEADS, 2, QK_NOPE_DIM)
    return w.transpose(0, 2, 1, 3).reshape(KV_LORA_RANK, 2 * V_WIDTH).astype(BF16)


def _down_weight(w_down):
    base = Q_LORA_RANK + KV_LORA_RANK
    x1 = w_down[:, base:base + ROPE_HALF]
    x2 = w_down[:, base + ROPE_HALF:]
    return jnp.concatenate([w_down[:, :base], x1, x1, x2, x2], axis=-1).astype(BF16)


def _pair_block_diag(w):
    w = w.reshape(LRU_HEADS // 2, 2, LRU_HEAD_DIM, LRU_HEAD_DIM)
    z = jnp.zeros_like(w[:, 0])
    top = jnp.concatenate([w[:, 0], z], axis=-1)
    bot = jnp.concatenate([z, w[:, 1]], axis=-1)
    return jnp.concatenate([top, bot], axis=-2)


def kernel(x, positions, ln_mix_g, ln_mix_b, ln_ffn_g, ln_ffn_b, even_w_in, pool_w, pool_scale, lru_conv_w, lru_conv_b, lru_w_a, lru_b_a, lru_w_x, lru_b_x, lru_lambda, even_w_out, mla_w_down, mla_q_norm_g, mla_kv_norm_g, mla_w_qb, mla_w_kvb, mla_w_o, mlp_w1, mlp_w2):
    row = lambda v: v.reshape(1, -1)
    inv_freq = ROPE_THETA ** (-jnp.arange(0, QK_ROPE_DIM, 2, dtype=F32) / QK_ROPE_DIM)
    invf = jnp.tile(inv_freq, LANES // ROPE_HALF).reshape(1, LANES)
    pos = positions.reshape(N_TOK, 1)

    xf = x.reshape(N_TOK, D_MODEL)
    for layer in range(DEPTH):
        j = layer // 2
        if layer % 2 == 0:
            y = _even_mix(
                xf.reshape(BATCH, SEQ, D_MODEL), even_w_in[j].astype(BF16), pool_w[j].astype(BF16),
                row(pool_scale[j]), lru_conv_w[j], row(lru_conv_b[j]),
                _pair_block_diag(0.5 * lru_w_a[j]).astype(BF16),
                _pair_block_diag(0.5 * lru_w_x[j]).astype(BF16),
                row(0.5 * lru_b_a[j]), row(0.5 * lru_b_x[j]), row(lru_lambda[j]))
            y = y.reshape(N_TOK, EVEN_MIX_WIDTH)
            w_out = even_w_out
        else:
            q, k, v = _mla_proj(
                xf, pos, invf, _down_weight(mla_w_down[j]),
                row(mla_q_norm_g[j]), row(mla_kv_norm_g[j]),
                _q_weight(mla_w_qb[j]), _kv_weight(mla_w_kvb[j]))
            y = _mla_attn(q.reshape(BATCH, SEQ, QK_WIDTH), k.reshape(BATCH, SEQ, QK_WIDTH),
                          v.reshape(BATCH, SEQ, V_WIDTH)).reshape(N_TOK, V_WIDTH)
            w_out = mla_w_o
        xf = _out_mlp(layer, y, xf, w_out, row(ln_mix_g[layer]), row(ln_mix_b[layer]),
                      mlp_w1, mlp_w2, row(ln_ffn_g[layer]), row(ln_ffn_b[layer]))
    return xf.reshape(BATCH, SEQ, D_MODEL)
```

```python
import functools

import numpy as np
import jax
import jax.numpy as jnp
from jax import lax
from jax.experimental import pallas as pl
from jax.experimental.pallas import tpu as pltpu

D_MODEL = 1024
BATCH = 8
SEQ = 2048
DEPTH = 4
N_TOK = BATCH * SEQ

CHUNK = 64

LANES = 128
SUBLANES = 8
assert BATCH == SUBLANES

POOL_WINDOWS = (2, 4, 8, 16)
POOL_GROUP_WIDTH = 128
POOL_WIDTH = 512
POOL_HIST = 16
LRU_WIDTH = 1024
LRU_HEADS = 8
LRU_HEAD_DIM = 128
CONV_WIDTH = 4
CONV_HIST = CONV_WIDTH - 1
LRU_C = 8.0
EVEN_IN_WIDTH = POOL_WIDTH + 2 * LRU_WIDTH
EVEN_MIX_WIDTH = POOL_WIDTH + LRU_WIDTH

MLA_HEADS = 8
QK_NOPE_DIM = 128
QK_ROPE_DIM = 64
ROPE_HALF = QK_ROPE_DIM // 2
V_HEAD_DIM = 128
Q_LORA_RANK = 384
KV_LORA_RANK = 256
ROPE_THETA = 10000.0
DOWN_WIDTH = Q_LORA_RANK + KV_LORA_RANK + LANES
QB_WIDTH = MLA_HEADS * QK_NOPE_DIM + (MLA_HEADS // 2) * LANES
QK_HEAD_WIDTH = 2 * LANES
QK_WIDTH = MLA_HEADS * QK_HEAD_WIDTH
V_WIDTH = MLA_HEADS * V_HEAD_DIM
ATTN_Q_ROWS = 512
ATTN_HEADS_PER_STEP = 4

D_FF = 4 * D_MODEL
FF_CHUNK = 512
DEEPNORM_ALPHA = (2 * DEPTH) ** 0.25
LN_EPS = 1e-5
RMS_EPS = 1e-6

MIX_FRAMES = 128
MIX_ROWS = MIX_FRAMES * BATCH
MIX_SUBTILES = 2
TOK_TILE = 1024
OUT_TILE = 512
STAGE = 512
STAGE_SLOTS = 4
VMEM_LIMIT = 60 * 1024 * 1024

F32 = jnp.float32
BF16 = jnp.bfloat16
F32_TINY = float(np.finfo(np.float32).tiny)


def _resident(shape):
    zeros = (0,) * len(shape)
    return pl.BlockSpec(shape, lambda *_: zeros, pipeline_mode=pl.Buffered(1))


def _layer_norm(v, g, b):
    mu = jnp.mean(v, axis=-1, keepdims=True)
    d = v - mu
    var = jnp.mean(d * d, axis=-1, keepdims=True)
    return d * lax.rsqrt(var + LN_EPS) * g + b


def _rms_norm(v, g):
    return v * lax.rsqrt(jnp.mean(v * v, axis=-1, keepdims=True) + RMS_EPS) * g


def _even_mix_kernel(x_ref, w_in_ref, pool_w_ref, pool_scale_ref, conv_w_ref, conv_b_ref,
                     wa_ref, wx_ref, ba_ref, bx_ref, lam_ref, y_ref,
                     pool_ext, lru_ext, carry, h_buf, x_tm, y_tm):
    t = pl.program_id(0)
    PH = POOL_HIST * BATCH
    CH = CONV_HIST * BATCH

    @pl.when(t == 0)
    def _():
        pool_ext[0:PH, :] = jnp.zeros((PH, POOL_WIDTH), F32)
        lru_ext[0:CH, :] = jnp.zeros((CH, LRU_WIDTH), F32)
        carry[...] = jnp.zeros((BATCH, LRU_WIDTH), F32)

    h = carry[...]
    for s in range(MIX_SUBTILES):
        h = _even_mix_rows(s, t, h, x_ref, w_in_ref, pool_w_ref, pool_scale_ref, conv_w_ref,
                           conv_b_ref, wa_ref, wx_ref, ba_ref, bx_ref, lam_ref, y_ref,
                           pool_ext, lru_ext, h_buf, x_tm, y_tm)
    carry[...] = h
    pool_ext[0:PH, :] = pool_ext[MIX_ROWS:MIX_ROWS + PH, :]
    lru_ext[0:CH, :] = lru_ext[MIX_ROWS:MIX_ROWS + CH, :]


def _even_mix_rows(s, t, h, x_ref, w_in_ref, pool_w_ref, pool_scale_ref, conv_w_ref, conv_b_ref,
                   wa_ref, wx_ref, ba_ref, bx_ref, lam_ref, y_ref, pool_ext, lru_ext, h_buf,
                   x_tm, y_tm):
    R = MIX_ROWS // MIX_SUBTILES
    F = MIX_FRAMES // MIX_SUBTILES
    r0, f0 = s * R, s * F
    PH = POOL_HIST * BATCH
    CH = CONV_HIST * BATCH

    for b in range(BATCH):
        for cs in range(D_MODEL // LANES):
            x_tm[cs, pl.ds(r0 + b, F, stride=BATCH), :] = (
                x_ref[b, f0:f0 + F, cs * LANES:(cs + 1) * LANES])
    xb = jnp.concatenate([x_tm[cs, r0:r0 + R, :] for cs in range(D_MODEL // LANES)],
                         axis=-1).astype(BF16)
    proj = jnp.dot(xb, w_in_ref[...], preferred_element_type=F32)
    u_pool = proj[:, :POOL_WIDTH]
    u_lru = proj[:, POOL_WIDTH:POOL_WIDTH + LRU_WIDTH]
    u_gate = proj[:, POOL_WIDTH + LRU_WIDTH:]
    pool_ext[PH + r0:PH + r0 + R, :] = u_pool
    lru_ext[CH + r0:CH + r0 + R, :] = u_lru

    frame = (lax.broadcasted_iota(jnp.int32, (R, 1), 0) + r0) // BATCH + t * MIX_FRAMES
    for g, w in enumerate(POOL_WINDOWS):
        lo, hi = g * POOL_GROUP_WIDTH, (g + 1) * POOL_GROUP_WIDTH
        acc = pool_ext[r0:r0 + PH + R, lo:hi]
        k = 1
        while k < w:
            acc = acc[k * BATCH:, :] + acc[:acc.shape[0] - k * BATCH, :]
            k *= 2
        acc = acc[acc.shape[0] - R:, :]
        count = jnp.minimum(frame + 1, w).astype(F32)
        d = acc / count - u_pool[:, lo:hi]
        yg = jnp.dot(d.astype(BF16), pool_w_ref[g], preferred_element_type=F32)
        y_tm[g, r0:r0 + R, :] = yg * pool_scale_ref[:, lo:hi]

    cw = conv_w_ref[...]
    c = cw[CONV_WIDTH - 1:CONV_WIDTH, :] * u_lru + conv_b_ref[...]
    for k in range(CONV_WIDTH - 1):
        start = r0 + CH - (CONV_WIDTH - 1 - k) * BATCH
        c = c + cw[k:k + 1, :] * lru_ext[start:start + R, :]

    cb = c.astype(BF16)
    za, zx = [], []
    for p in range(LRU_HEADS // 2):
        cp = cb[:, p * 2 * LRU_HEAD_DIM:(p + 1) * 2 * LRU_HEAD_DIM]
        za.append(jnp.dot(cp, wa_ref[p], preferred_element_type=F32))
        zx.append(jnp.dot(cp, wx_ref[p], preferred_element_type=F32))
    ta = jnp.tanh(jnp.concatenate(za, axis=-1) + ba_ref[...])
    tx = jnp.tanh(jnp.concatenate(zx, axis=-1) + bx_ref[...])

    z = -lam_ref[...]
    softplus = jnp.maximum(z, 0.0) + jnp.log1p(jnp.exp(-jnp.abs(z)))
    half_rate = (-0.5 * LRU_C) * softplus
    log_a = ta * half_rate + half_rate
    a = jnp.exp(log_a)
    m2 = jnp.tanh(-log_a) * (a * a + 1.0)
    mult = m2 * lax.rsqrt(jnp.maximum(m2, F32_TINY))
    half_c = 0.5 * c
    xin = mult * (tx * half_c + half_c)

    for j in range(R // BATCH):
        rows = slice(j * BATCH, (j + 1) * BATCH)
        h = a[rows, :] * h + xin[rows, :]
        h_buf[r0 + j * BATCH:r0 + (j + 1) * BATCH, :] = h

    y_lru = h_buf[r0:r0 + R, :] * jax.nn.gelu(u_gate)
    n_pool = len(POOL_WINDOWS)
    for cs in range(LRU_WIDTH // LANES):
        y_tm[n_pool + cs, r0:r0 + R, :] = y_lru[:, cs * LANES:(cs + 1) * LANES]
    for b in range(BATCH):
        for cs in range(EVEN_MIX_WIDTH // LANES):
            y_ref[b, f0:f0 + F, cs * LANES:(cs + 1) * LANES] = (
                y_tm[cs, pl.ds(r0 + b, F, stride=BATCH), :].astype(BF16))
    return h


def _even_mix(x, w_in, pool_w, pool_scale, conv_w, conv_b, wa, wx, ba, bx, lam):
    R = MIX_ROWS
    return pl.pallas_call(
        _even_mix_kernel,
        grid=(SEQ // MIX_FRAMES,),
        in_specs=[
            pl.BlockSpec((BATCH, MIX_FRAMES, D_MODEL), lambda t: (0, t, 0)),
            _resident((D_MODEL, EVEN_IN_WIDTH)),
            _resident((len(POOL_WINDOWS), POOL_GROUP_WIDTH, POOL_GROUP_WIDTH)),
            _resident((1, POOL_WIDTH)),
            _resident((CONV_WIDTH, LRU_WIDTH)),
            _resident((1, LRU_WIDTH)),
            _resident((LRU_HEADS // 2, 2 * LRU_HEAD_DIM, 2 * LRU_HEAD_DIM)),
            _resident((LRU_HEADS // 2, 2 * LRU_HEAD_DIM, 2 * LRU_HEAD_DIM)),
            _resident((1, LRU_WIDTH)),
            _resident((1, LRU_WIDTH)),
            _resident((1, LRU_WIDTH)),
        ],
        out_specs=pl.BlockSpec((BATCH, MIX_FRAMES, EVEN_MIX_WIDTH), lambda t: (0, t, 0)),
        out_shape=jax.ShapeDtypeStruct((BATCH, SEQ, EVEN_MIX_WIDTH), BF16),
        scratch_shapes=[
            pltpu.VMEM((R + POOL_HIST * BATCH, POOL_WIDTH), F32),
            pltpu.VMEM((R + CONV_HIST * BATCH, LRU_WIDTH), F32),
            pltpu.VMEM((BATCH, LRU_WIDTH), F32),
            pltpu.VMEM((R, LRU_WIDTH), F32),
            pltpu.VMEM((D_MODEL // LANES, R, LANES), F32),
            pltpu.VMEM((EVEN_MIX_WIDTH // LANES, R, LANES), F32),
        ],
        compiler_params=pltpu.CompilerParams(
            dimension_semantics=("arbitrary",),
            vmem_limit_bytes=VMEM_LIMIT),
        name="even_mix",
    )(x, w_in, pool_w, pool_scale, conv_w, conv_b, wa, wx, ba, bx, lam)


def _weight_pieces(src, dst, rows, cols):
    return [(src.at[r:r + STAGE, c:c + STAGE], dst.at[r:r + STAGE, c:c + STAGE])
            for r in range(rows.start, rows.stop, STAGE) for c in range(cols.start, cols.stop, STAGE)]


class _WeightStream:
    def __init__(self, pieces, stage, sem):
        self.pieces, self.stage, self.sem, self.done = pieces, stage, sem, 0
        for i in range(min(STAGE_SLOTS, len(pieces))):
            self._copy(i).start()

    def _copy(self, i):
        slot = i % STAGE_SLOTS
        return pltpu.make_async_copy(self.pieces[i][0], self.stage.at[slot], self.sem.at[slot])

    def need(self, n):
        for i in range(self.done, n):
            self._copy(i).wait()
            self.pieces[i][1][...] = self.stage[i % STAGE_SLOTS].astype(BF16)
            if i + STAGE_SLOTS < len(self.pieces):
                self._copy(i + STAGE_SLOTS).start()
        self.done = max(self.done, n)


def _out_mlp_kernel(layer, y_ref, x_ref, w_out_hbm, g1_ref, b1_ref, w1_hbm, w2_hbm, g2_ref, b2_ref,
                    o_ref, w_out_ref, w1_ref, w2_ref, stage, sem):
    kd = w_out_ref.shape[0]
    n_ff = D_FF // FF_CHUNK

    def tile(stream):
        n_out = (kd // STAGE) * (D_MODEL // STAGE)
        per_ff = 2 * (D_MODEL // STAGE) * (FF_CHUNK // STAGE)
        if stream is not None:
            stream.need(n_out)
        mix = jnp.dot(y_ref[...], w_out_ref[...], preferred_element_type=F32)
        x1 = _layer_norm(DEEPNORM_ALPHA * x_ref[...] + mix, g1_ref[...], b1_ref[...])
        xb = x1.astype(BF16)
        acc = None
        for c in range(n_ff):
            lo, hi = c * FF_CHUNK, (c + 1) * FF_CHUNK
            if stream is not None:
                stream.need(n_out + (c + 1) * per_ff)
            h = jnp.dot(xb, w1_ref[:, lo:hi], preferred_element_type=F32)
            h = jnp.square(jnp.maximum(h, 0.0)).astype(BF16)
            part = jnp.dot(h, w2_ref[lo:hi, :], preferred_element_type=F32)
            acc = part if acc is None else acc + part
        o_ref[...] = _layer_norm(DEEPNORM_ALPHA * x1 + acc, g2_ref[...], b2_ref[...])

    @pl.when(pl.program_id(0) == 0)
    def _():
        pieces = _weight_pieces(w_out_hbm.at[layer // 2], w_out_ref, slice(0, kd), slice(0, D_MODEL))
        for c in range(n_ff):
            ff = slice(c * FF_CHUNK, (c + 1) * FF_CHUNK)
            pieces += _weight_pieces(w1_hbm.at[layer], w1_ref, slice(0, D_MODEL), ff)
            pieces += _weight_pieces(w2_hbm.at[layer], w2_ref, ff, slice(0, D_MODEL))
        tile(_WeightStream(pieces, stage, sem))

    @pl.when(pl.program_id(0) != 0)
    def _():
        tile(None)


def _out_mlp(layer, y, x, w_out_all, g1, b1, w1_all, w2_all, g2, b2):
    T = OUT_TILE
    kd = y.shape[-1]
    hbm = pl.BlockSpec(memory_space=pl.ANY)
    return pl.pallas_call(
        functools.partial(_out_mlp_kernel, layer),
        grid=(N_TOK // T,),
        in_specs=[
            pl.BlockSpec((T, kd), lambda i: (i, 0)),
            pl.BlockSpec((T, D_MODEL), lambda i: (i, 0)),
            hbm,
            _resident((1, D_MODEL)),
            _resident((1, D_MODEL)),
            hbm,
            hbm,
            _resident((1, D_MODEL)),
            _resident((1, D_MODEL)),
        ],
        out_specs=pl.BlockSpec((T, D_MODEL), lambda i: (i, 0)),
        out_shape=jax.ShapeDtypeStruct((N_TOK, D_MODEL), F32),
        scratch_shapes=[
            pltpu.VMEM((kd, D_MODEL), BF16),
            pltpu.VMEM((D_MODEL, D_FF), BF16),
            pltpu.VMEM((D_FF, D_MODEL), BF16),
            pltpu.VMEM((STAGE_SLOTS, STAGE, STAGE), F32),
            pltpu.SemaphoreType.DMA((STAGE_SLOTS,)),
        ],
        compiler_params=pltpu.CompilerParams(
            dimension_semantics=("arbitrary",),
            vmem_limit_bytes=VMEM_LIMIT),
        name="out_mlp",
    )(y, x, w_out_all, g1, b1, w1_all, w2_all, g2, b2)


def _mla_proj_kernel(x_ref, pos_ref, invf_ref, w_down_ref, qg_ref, kvg_ref, w_qb_ref, w_kvb_ref,
                     q_ref, k_ref, v_ref):
    xb = x_ref[...].astype(BF16)
    down = jnp.dot(xb, w_down_ref[...], preferred_element_type=F32)
    cq = _rms_norm(down[:, :Q_LORA_RANK], qg_ref[...])
    ckv = _rms_norm(down[:, Q_LORA_RANK:Q_LORA_RANK + KV_LORA_RANK], kvg_ref[...])
    kpe = down[:, Q_LORA_RANK + KV_LORA_RANK:]

    ang = pos_ref[...].astype(F32) * invf_ref[...]
    cos = jnp.cos(ang)
    lane = lax.broadcasted_iota(jnp.int32, (1, LANES), 1)
    sin = jnp.sin(ang) * jnp.where(lane < 2 * ROPE_HALF, -1.0, 1.0)

    def rope(blk):
        return blk * cos + pltpu.roll(blk, 2 * ROPE_HALF, axis=1) * sin

    q = jnp.dot(cq.astype(BF16), w_qb_ref[...], preferred_element_type=F32)
    n_nope = MLA_HEADS * QK_NOPE_DIM
    q_pe = [rope(q[:, n_nope + p * LANES:n_nope + (p + 1) * LANES]).astype(BF16)
            for p in range(MLA_HEADS // 2)]
    kv = jnp.dot(ckv.astype(BF16), w_kvb_ref[...], preferred_element_type=F32)
    kr = rope(kpe)
    first_slot = (lane % (2 * ROPE_HALF)) < ROPE_HALF
    k_pe = [jnp.where(first_slot, kr, 0.0).astype(BF16),
            jnp.where(first_slot, 0.0, kr).astype(BF16)]
    for h in range(MLA_HEADS):
        lo = h * QK_HEAD_WIDTH
        q_ref[:, lo:lo + LANES] = q[:, h * LANES:(h + 1) * LANES].astype(BF16)
        q_ref[:, lo + LANES:lo + 2 * LANES] = q_pe[h // 2]
        k_ref[:, lo:lo + LANES] = kv[:, h * LANES:(h + 1) * LANES].astype(BF16)
        k_ref[:, lo + LANES:lo + 2 * LANES] = k_pe[h % 2]
    v_ref[...] = kv[:, n_nope:].astype(BF16)


def _mla_proj(x, pos, invf, w_down, qg, kvg, w_qb, w_kvb):
    T = TOK_TILE
    return pl.pallas_call(
        _mla_proj_kernel,
        grid=(N_TOK // T,),
        in_specs=[
            pl.BlockSpec((T, D_MODEL), lambda i: (i, 0)),
            pl.BlockSpec((T, 1), lambda i: (i, 0)),
            _resident((1, LANES)),
            _resident((D_MODEL, DOWN_WIDTH)),
            _resident((1, Q_LORA_RANK)),
            _resident((1, KV_LORA_RANK)),
            _resident((Q_LORA_RANK, QB_WIDTH)),
            _resident((KV_LORA_RANK, 2 * V_WIDTH)),
        ],
        out_specs=[
            pl.BlockSpec((T, QK_WIDTH), lambda i: (i, 0)),
            pl.BlockSpec((T, QK_WIDTH), lambda i: (i, 0)),
            pl.BlockSpec((T, V_WIDTH), lambda i: (i, 0)),
        ],
        out_shape=[
            jax.ShapeDtypeStruct((N_TOK, QK_WIDTH), BF16),
            jax.ShapeDtypeStruct((N_TOK, QK_WIDTH), BF16),
            jax.ShapeDtypeStruct((N_TOK, V_WIDTH), BF16),
        ],
        compiler_params=pltpu.CompilerParams(
            dimension_semantics=("arbitrary",),
            vmem_limit_bytes=VMEM_LIMIT),
        name="mla_proj",
    )(x, pos, invf, w_down, qg, kvg, w_qb, w_kvb)


_NT = (((1,), (1,)), ((), ()))


def _mla_attn_kernel(q_ref, k_ref, v_ref, o_ref, v_ones):
    c = (QK_NOPE_DIM + QK_ROPE_DIM) ** -0.5 * float(np.log2(np.e))
    neg = jnp.finfo(F32).min
    R = ATTN_Q_ROWS
    row = lax.broadcasted_iota(jnp.int32, (R, R), 0)
    col = lax.broadcasted_iota(jnp.int32, (R, R), 1)
    hidden = (col // CHUNK) > (row // CHUNK)
    for hh in range(ATTN_HEADS_PER_STEP):
        v_ones[hh, :, :V_HEAD_DIM] = v_ref[:, hh * V_HEAD_DIM:(hh + 1) * V_HEAD_DIM]
        v_ones[hh, :, V_HEAD_DIM:] = jnp.ones((SEQ, LANES), BF16)
    for j in range(SEQ // R):
        qs, ke = j * R, (j + 1) * R
        for hh in range(ATTN_HEADS_PER_STEP):
            lo, hi = hh * QK_HEAD_WIDTH, (hh + 1) * QK_HEAD_WIDTH
            s = lax.dot_general(q_ref[qs:ke, lo:hi], k_ref[:ke, lo:hi], _NT,
                                preferred_element_type=F32) * c
            diag = jnp.where(hidden, neg, s[:, qs:ke])
            s = diag if j == 0 else jnp.concatenate([s[:, :qs], diag], axis=-1)
            m = jnp.max(s, axis=-1, keepdims=True)
            p = jnp.exp2(s - m).astype(BF16)
            ol = jnp.dot(p, v_ones[hh, :ke, :], preferred_element_type=F32)
            o = ol[:, :V_HEAD_DIM] / ol[:, V_HEAD_DIM:V_HEAD_DIM + 1]
            o_ref[qs:ke, hh * V_HEAD_DIM:(hh + 1) * V_HEAD_DIM] = o.astype(BF16)


def _mla_attn(q, k, v):
    hps = ATTN_HEADS_PER_STEP
    blk = lambda b, g: (b, 0, g)
    return pl.pallas_call(
        _mla_attn_kernel,
        grid=(BATCH, MLA_HEADS // hps),
        in_specs=[
            pl.BlockSpec((None, SEQ, hps * QK_HEAD_WIDTH), blk),
            pl.BlockSpec((None, SEQ, hps * QK_HEAD_WIDTH), blk),
            pl.BlockSpec((None, SEQ, hps * V_HEAD_DIM), blk),
        ],
        out_specs=pl.BlockSpec((None, SEQ, hps * V_HEAD_DIM), blk),
        out_shape=jax.ShapeDtypeStruct((BATCH, SEQ, V_WIDTH), BF16),
        scratch_shapes=[pltpu.VMEM((hps, SEQ, V_HEAD_DIM + LANES), BF16)],
        compiler_params=pltpu.CompilerParams(
            dimension_semantics=("arbitrary", "arbitrary"),
            vmem_limit_bytes=VMEM_LIMIT),
        name="mla_attn",
    )(q, k, v)


def _q_weight(w_qb):
    w = w_qb.reshape(Q_LORA_RANK, MLA_HEADS, QK_NOPE_DIM + QK_ROPE_DIM)
    nope = w[:, :, :QK_NOPE_DIM].reshape(Q_LORA_RANK, MLA_HEADS * QK_NOPE_DIM)
    pe = w[:, :, QK_NOPE_DIM:].reshape(Q_LORA_RANK, MLA_HEADS // 2, 2, 2, ROPE_HALF)
    pe = pe.transpose(0, 1, 3, 2, 4).reshape(Q_LORA_RANK, (MLA_HEADS // 2) * LANES)
    return jnp.concatenate([nope, pe], axis=-1).astype(BF16)


def _kv_weight(w_kvb):
    w = w_kvb.reshape(KV_LORA_RANK, MLA_HEADS, 2, QK_NOPE_DIM)
    return w.transpose(0, 2, 1, 3).reshape(KV_LORA_RANK, 2 * V_WIDTH).astype(BF16)


def _down_weight(w_down):
    base = Q_LORA_RANK + KV_LORA_RANK
    x1 = w_down[:, base:base + ROPE_HALF]
    x2 = w_down[:, base + ROPE_HALF:]
    return jnp.concatenate([w_down[:, :base], x1, x1, x2, x2], axis=-1).astype(BF16)


def _pair_block_diag(w):
    w = w.reshape(LRU_HEADS // 2, 2, LRU_HEAD_DIM, LRU_HEAD_DIM)
    z = jnp.zeros_like(w[:, 0])
    top = jnp.concatenate([w[:, 0], z], axis=-1)
    bot = jnp.concatenate([z, w[:, 1]], axis=-1)
    return jnp.concatenate([top, bot], axis=-2)


def kernel(x, positions, ln_mix_g, ln_mix_b, ln_ffn_g, ln_ffn_b, even_w_in, pool_w, pool_scale, lru_conv_w, lru_conv_b, lru_w_a, lru_b_a, lru_w_x, lru_b_x, lru_lambda, even_w_out, mla_w_down, mla_q_norm_g, mla_kv_norm_g, mla_w_qb, mla_w_kvb, mla_w_o, mlp_w1, mlp_w2):
    row = lambda v: v.reshape(1, -1)
    inv_freq = ROPE_THETA ** (-jnp.arange(0, QK_ROPE_DIM, 2, dtype=F32) / QK_ROPE_DIM)
    invf = jnp.tile(inv_freq, LANES // ROPE_HALF).reshape(1, LANES)
    pos = positions.reshape(N_TOK, 1)

    xf = x.reshape(N_TOK, D_MODEL)
    for layer in range(DEPTH):
        j = layer // 2
        if layer % 2 == 0:
            y = _even_mix(
                xf.reshape(BATCH, SEQ, D_MODEL), even_w_in[j].astype(BF16), pool_w[j].astype(BF16),
                row(pool_scale[j]), lru_conv_w[j], row(lru_conv_b[j]),
                _pair_block_diag(0.5 * lru_w_a[j]).astype(BF16),
                _pair_block_diag(0.5 * lru_w_x[j]).astype(BF16),
                row(0.5 * lru_b_a[j]), row(0.5 * lru_b_x[j]), row(lru_lambda[j]))
            y = y.reshape(N_TOK, EVEN_MIX_WIDTH)
            w_out = even_w_out
        else:
            q, k, v = _mla_proj(
                xf, pos, invf, _down_weight(mla_w_down[j]),
                row(mla_q_norm_g[j]), row(mla_kv_norm_g[j]),
                _q_weight(mla_w_qb[j]), _kv_weight(mla_w_kvb[j]))
            y = _mla_attn(q.reshape(BATCH, SEQ, QK_WIDTH), k.reshape(BATCH, SEQ, QK_WIDTH),
                          v.reshape(BATCH, SEQ, V_WIDTH)).reshape(N_TOK, V_WIDTH)
            w_out = mla_w_o
        xf = _out_mlp(layer, y, xf, w_out, row(ln_mix_g[layer]), row(ln_mix_b[layer]),
                      mlp_w1, mlp_w2, row(ln_ffn_g[layer]), row(ln_ffn_b[layer]))
    return xf.reshape(BATCH, SEQ, D_MODEL)
```

```python
import functools

import numpy as np
import jax
import jax.numpy as jnp
from jax import lax
from jax.experimental import pallas as pl
from jax.experimental.pallas import tpu as pltpu

D_MODEL = 1024
BATCH = 8
SEQ = 2048
DEPTH = 4
N_TOK = BATCH * SEQ

CHUNK = 64

LANES = 128
SUBLANES = 8
assert BATCH == SUBLANES

POOL_WINDOWS = (2, 4, 8, 16)
POOL_GROUP_WIDTH = 128
POOL_WIDTH = 512
POOL_HIST = 16
LRU_WIDTH = 1024
LRU_HEADS = 8
LRU_HEAD_DIM = 128
CONV_WIDTH = 4
CONV_HIST = CONV_WIDTH - 1
LRU_C = 8.0
EVEN_IN_WIDTH = POOL_WIDTH + 2 * LRU_WIDTH
EVEN_MIX_WIDTH = POOL_WIDTH + LRU_WIDTH

MLA_HEADS = 8
QK_NOPE_DIM = 128
QK_ROPE_DIM = 64
ROPE_HALF = QK_ROPE_DIM // 2
V_HEAD_DIM = 128
Q_LORA_RANK = 384
KV_LORA_RANK = 256
ROPE_THETA = 10000.0
DOWN_WIDTH = Q_LORA_RANK + KV_LORA_RANK + LANES
QB_WIDTH = MLA_HEADS * QK_NOPE_DIM + (MLA_HEADS // 2) * LANES
QK_HEAD_WIDTH = 2 * LANES
QK_WIDTH = MLA_HEADS * QK_HEAD_WIDTH
V_WIDTH = MLA_HEADS * V_HEAD_DIM
ATTN_Q_ROWS = 512
ATTN_HEADS_PER_STEP = 4

D_FF = 4 * D_MODEL
FF_CHUNK = 512
DEEPNORM_ALPHA = (2 * DEPTH) ** 0.25
LN_EPS = 1e-5
RMS_EPS = 1e-6

MIX_FRAMES = 128
MIX_ROWS = MIX_FRAMES * BATCH
MIX_SUBTILES = 1
TOK_TILE = 1024
STAGE_ROWS = 512
VMEM_LIMIT = 60 * 1024 * 1024

F32 = jnp.float32
BF16 = jnp.bfloat16
F32_TINY = float(np.finfo(np.float32).tiny)


def _resident(shape):
    zeros = (0,) * len(shape)
    return pl.BlockSpec(shape, lambda *_: zeros, pipeline_mode=pl.Buffered(1))


def _layer_norm(v, g, b):
    mu = jnp.mean(v, axis=-1, keepdims=True)
    d = v - mu
    var = jnp.mean(d * d, axis=-1, keepdims=True)
    return d * lax.rsqrt(var + LN_EPS) * g + b


def _rms_norm(v, g):
    return v * lax.rsqrt(jnp.mean(v * v, axis=-1, keepdims=True) + RMS_EPS) * g


def _even_mix_kernel(x_ref, w_in_ref, pool_w_ref, pool_scale_ref, conv_w_ref, conv_b_ref,
                     wa_ref, wx_ref, ba_ref, bx_ref, lam_ref, y_ref,
                     pool_ext, lru_ext, carry, h_buf, x_tm, y_tm):
    t = pl.program_id(0)
    PH = POOL_HIST * BATCH
    CH = CONV_HIST * BATCH

    @pl.when(t == 0)
    def _():
        pool_ext[0:PH, :] = jnp.zeros((PH, POOL_WIDTH), F32)
        lru_ext[0:CH, :] = jnp.zeros((CH, LRU_WIDTH), F32)
        carry[...] = jnp.zeros((BATCH, LRU_WIDTH), F32)

    h = carry[...]
    for s in range(MIX_SUBTILES):
        h = _even_mix_rows(s, t, h, x_ref, w_in_ref, pool_w_ref, pool_scale_ref, conv_w_ref,
                           conv_b_ref, wa_ref, wx_ref, ba_ref, bx_ref, lam_ref, y_ref,
                           pool_ext, lru_ext, h_buf, x_tm, y_tm)
    carry[...] = h
    pool_ext[0:PH, :] = pool_ext[MIX_ROWS:MIX_ROWS + PH, :]
    lru_ext[0:CH, :] = lru_ext[MIX_ROWS:MIX_ROWS + CH, :]


def _even_mix_rows(s, t, h, x_ref, w_in_ref, pool_w_ref, pool_scale_ref, conv_w_ref, conv_b_ref,
                   wa_ref, wx_ref, ba_ref, bx_ref, lam_ref, y_ref, pool_ext, lru_ext, h_buf,
                   x_tm, y_tm):
    R = MIX_ROWS // MIX_SUBTILES
    F = MIX_FRAMES // MIX_SUBTILES
    r0, f0 = s * R, s * F
    PH = POOL_HIST * BATCH
    CH = CONV_HIST * BATCH

    for b in range(BATCH):
        for cs in range(D_MODEL // LANES):
            x_tm[cs, pl.ds(r0 + b, F, stride=BATCH), :] = (
                x_ref[b, f0:f0 + F, cs * LANES:(cs + 1) * LANES])
    xb = jnp.concatenate([x_tm[cs, r0:r0 + R, :] for cs in range(D_MODEL // LANES)],
                         axis=-1).astype(BF16)
    proj = jnp.dot(xb, w_in_ref[...], preferred_element_type=F32)
    u_pool = proj[:, :POOL_WIDTH]
    u_lru = proj[:, POOL_WIDTH:POOL_WIDTH + LRU_WIDTH]
    u_gate = proj[:, POOL_WIDTH + LRU_WIDTH:]
    pool_ext[PH + r0:PH + r0 + R, :] = u_pool
    lru_ext[CH + r0:CH + r0 + R, :] = u_lru

    frame = (lax.broadcasted_iota(jnp.int32, (R, 1), 0) + r0) // BATCH + t * MIX_FRAMES
    for g, w in enumerate(POOL_WINDOWS):
        lo, hi = g * POOL_GROUP_WIDTH, (g + 1) * POOL_GROUP_WIDTH
        acc = pool_ext[r0:r0 + PH + R, lo:hi]
        k = 1
        while k < w:
            acc = acc[k * BATCH:, :] + acc[:acc.shape[0] - k * BATCH, :]
            k *= 2
        acc = acc[acc.shape[0] - R:, :]
        count = jnp.minimum(frame + 1, w).astype(F32)
        d = acc / count - u_pool[:, lo:hi]
        yg = jnp.dot(d.astype(BF16), pool_w_ref[g], preferred_element_type=F32)
        y_tm[g, r0:r0 + R, :] = yg * pool_scale_ref[:, lo:hi]

    cw = conv_w_ref[...]
    c = cw[CONV_WIDTH - 1:CONV_WIDTH, :] * u_lru + conv_b_ref[...]
    for k in range(CONV_WIDTH - 1):
        start = r0 + CH - (CONV_WIDTH - 1 - k) * BATCH
        c = c + cw[k:k + 1, :] * lru_ext[start:start + R, :]

    cb = c.astype(BF16)
    za, zx = [], []
    for p in range(LRU_HEADS // 2):
        cp = cb[:, p * 2 * LRU_HEAD_DIM:(p + 1) * 2 * LRU_HEAD_DIM]
        za.append(jnp.dot(cp, wa_ref[p], preferred_element_type=F32))
        zx.append(jnp.dot(cp, wx_ref[p], preferred_element_type=F32))
    ta = jnp.tanh(jnp.concatenate(za, axis=-1) + ba_ref[...])
    tx = jnp.tanh(jnp.concatenate(zx, axis=-1) + bx_ref[...])

    z = -lam_ref[...]
    softplus = jnp.maximum(z, 0.0) + jnp.log1p(jnp.exp(-jnp.abs(z)))
    half_rate = (-0.5 * LRU_C) * softplus
    log_a = ta * half_rate + half_rate
    a = jnp.exp(log_a)
    m2 = jnp.tanh(-log_a) * (a * a + 1.0)
    mult = m2 * lax.rsqrt(jnp.maximum(m2, F32_TINY))
    half_c = 0.5 * c
    xin = mult * (tx * half_c + half_c)

    for j in range(R // BATCH):
        rows = slice(j * BATCH, (j + 1) * BATCH)
        h = a[rows, :] * h + xin[rows, :]
        h_buf[r0 + j * BATCH:r0 + (j + 1) * BATCH, :] = h

    y_lru = h_buf[r0:r0 + R, :] * jax.nn.gelu(u_gate)
    n_pool = len(POOL_WINDOWS)
    for cs in range(LRU_WIDTH // LANES):
        y_tm[n_pool + cs, r0:r0 + R, :] = y_lru[:, cs * LANES:(cs + 1) * LANES]
    for b in range(BATCH):
        for cs in range(EVEN_MIX_WIDTH // LANES):
            y_ref[b, f0:f0 + F, cs * LANES:(cs + 1) * LANES] = (
                y_tm[cs, pl.ds(r0 + b, F, stride=BATCH), :].astype(BF16))
    return h


def _even_mix(x, w_in, pool_w, pool_scale, conv_w, conv_b, wa, wx, ba, bx, lam):
    R = MIX_ROWS
    return pl.pallas_call(
        _even_mix_kernel,
        grid=(SEQ // MIX_FRAMES,),
        in_specs=[
            pl.BlockSpec((BATCH, MIX_FRAMES, D_MODEL), lambda t: (0, t, 0)),
            _resident((D_MODEL, EVEN_IN_WIDTH)),
            _resident((len(POOL_WINDOWS), POOL_GROUP_WIDTH, POOL_GROUP_WIDTH)),
            _resident((1, POOL_WIDTH)),
            _resident((CONV_WIDTH, LRU_WIDTH)),
            _resident((1, LRU_WIDTH)),
            _resident((LRU_HEADS // 2, 2 * LRU_HEAD_DIM, 2 * LRU_HEAD_DIM)),
            _resident((LRU_HEADS // 2, 2 * LRU_HEAD_DIM, 2 * LRU_HEAD_DIM)),
            _resident((1, LRU_WIDTH)),
            _resident((1, LRU_WIDTH)),
            _resident((1, LRU_WIDTH)),
        ],
        out_specs=pl.BlockSpec((BATCH, MIX_FRAMES, EVEN_MIX_WIDTH), lambda t: (0, t, 0)),
        out_shape=jax.ShapeDtypeStruct((BATCH, SEQ, EVEN_MIX_WIDTH), BF16),
        scratch_shapes=[
            pltpu.VMEM((R + POOL_HIST * BATCH, POOL_WIDTH), F32),
            pltpu.VMEM((R + CONV_HIST * BATCH, LRU_WIDTH), F32),
            pltpu.VMEM((BATCH, LRU_WIDTH), F32),
            pltpu.VMEM((R, LRU_WIDTH), F32),
            pltpu.VMEM((D_MODEL // LANES, R, LANES), F32),
            pltpu.VMEM((EVEN_MIX_WIDTH // LANES, R, LANES), F32),
        ],
        compiler_params=pltpu.CompilerParams(
            dimension_semantics=("arbitrary",),
            vmem_limit_bytes=VMEM_LIMIT),
        name="even_mix",
    )(x, w_in, pool_w, pool_scale, conv_w, conv_b, wa, wx, ba, bx, lam)


def _weight_chunks(src, dst):
    rows, cols = dst.shape
    return [(src.at[r:r + STAGE_ROWS, c:c + D_MODEL], dst.at[r:r + STAGE_ROWS, c:c + D_MODEL])
            for r in range(0, rows, STAGE_ROWS) for c in range(0, cols, D_MODEL)]


def _load_weights_as_bf16(chunks, stage, sem):
    copy = lambda i: pltpu.make_async_copy(chunks[i][0], stage.at[i % 2], sem.at[i % 2])
    copy(0).start()
    for i in range(len(chunks)):
        if i + 1 < len(chunks):
            copy(i + 1).start()
        copy(i).wait()
        chunks[i][1][...] = stage[i % 2].astype(BF16)


def _out_mlp_kernel(layer, y_ref, x_ref, w_out_hbm, g1_ref, b1_ref, w1_hbm, w2_hbm, g2_ref, b2_ref,
                    o_ref, w_out_ref, w1_ref, w2_ref, stage, sem):
    @pl.when(pl.program_id(0) == 0)
    def _():
        _load_weights_as_bf16(
            _weight_chunks(w_out_hbm.at[layer // 2], w_out_ref)
            + _weight_chunks(w1_hbm.at[layer], w1_ref)
            + _weight_chunks(w2_hbm.at[layer], w2_ref), stage, sem)

    mix = jnp.dot(y_ref[...], w_out_ref[...], preferred_element_type=F32)
    x1 = _layer_norm(DEEPNORM_ALPHA * x_ref[...] + mix, g1_ref[...], b1_ref[...])
    xb = x1.astype(BF16)
    acc = None
    for c in range(D_FF // FF_CHUNK):
        lo, hi = c * FF_CHUNK, (c + 1) * FF_CHUNK
        h = jnp.dot(xb, w1_ref[:, lo:hi], preferred_element_type=F32)
        h = jnp.square(jnp.maximum(h, 0.0)).astype(BF16)
        part = jnp.dot(h, w2_ref[lo:hi, :], preferred_element_type=F32)
        acc = part if acc is None else acc + part
    o_ref[...] = _layer_norm(DEEPNORM_ALPHA * x1 + acc, g2_ref[...], b2_ref[...])


def _out_mlp(layer, y, x, w_out_all, g1, b1, w1_all, w2_all, g2, b2):
    T = TOK_TILE
    kd = y.shape[-1]
    hbm = pl.BlockSpec(memory_space=pl.ANY)
    return pl.pallas_call(
        functools.partial(_out_mlp_kernel, layer),
        grid=(N_TOK // T,),
        in_specs=[
            pl.BlockSpec((T, kd), lambda i: (i, 0)),
            pl.BlockSpec((T, D_MODEL), lambda i: (i, 0)),
            hbm,
            _resident((1, D_MODEL)),
            _resident((1, D_MODEL)),
            hbm,
            hbm,
            _resident((1, D_MODEL)),
            _resident((1, D_MODEL)),
        ],
        out_specs=pl.BlockSpec((T, D_MODEL), lambda i: (i, 0)),
        out_shape=jax.ShapeDtypeStruct((N_TOK, D_MODEL), F32),
        scratch_shapes=[
            pltpu.VMEM((kd, D_MODEL), BF16),
            pltpu.VMEM((D_MODEL, D_FF), BF16),
            pltpu.VMEM((D_FF, D_MODEL), BF16),
            pltpu.VMEM((2, STAGE_ROWS, D_MODEL), F32),
            pltpu.SemaphoreType.DMA((2,)),
        ],
        compiler_params=pltpu.CompilerParams(
            dimension_semantics=("arbitrary",),
            vmem_limit_bytes=VMEM_LIMIT),
        name="out_mlp",
    )(y, x, w_out_all, g1, b1, w1_all, w2_all, g2, b2)


def _mla_proj_kernel(x_ref, pos_ref, invf_ref, w_down_ref, qg_ref, kvg_ref, w_qb_ref, w_kvb_ref,
                     q_ref, k_ref, v_ref):
    xb = x_ref[...].astype(BF16)
    down = jnp.dot(xb, w_down_ref[...], preferred_element_type=F32)
    cq = _rms_norm(down[:, :Q_LORA_RANK], qg_ref[...])
    ckv = _rms_norm(down[:, Q_LORA_RANK:Q_LORA_RANK + KV_LORA_RANK], kvg_ref[...])
    kpe = down[:, Q_LORA_RANK + KV_LORA_RANK:]

    ang = pos_ref[...].astype(F32) * invf_ref[...]
    cos = jnp.cos(ang)
    lane = lax.broadcasted_iota(jnp.int32, (1, LANES), 1)
    sin = jnp.sin(ang) * jnp.where(lane < 2 * ROPE_HALF, -1.0, 1.0)

    def rope(blk):
        return blk * cos + pltpu.roll(blk, 2 * ROPE_HALF, axis=1) * sin

    q = jnp.dot(cq.astype(BF16), w_qb_ref[...], preferred_element_type=F32)
    n_nope = MLA_HEADS * QK_NOPE_DIM
    q_pe = [rope(q[:, n_nope + p * LANES:n_nope + (p + 1) * LANES]).astype(BF16)
            for p in range(MLA_HEADS // 2)]
    kv = jnp.dot(ckv.astype(BF16), w_kvb_ref[...], preferred_element_type=F32)
    kr = rope(kpe)
    first_slot = (lane % (2 * ROPE_HALF)) < ROPE_HALF
    k_pe = [jnp.where(first_slot, kr, 0.0).astype(BF16),
            jnp.where(first_slot, 0.0, kr).astype(BF16)]
    for h in range(MLA_HEADS):
        lo = h * QK_HEAD_WIDTH
        q_ref[:, lo:lo + LANES] = q[:, h * LANES:(h + 1) * LANES].astype(BF16)
        q_ref[:, lo + LANES:lo + 2 * LANES] = q_pe[h // 2]
        k_ref[:, lo:lo + LANES] = kv[:, h * LANES:(h + 1) * LANES].astype(BF16)
        k_ref[:, lo + LANES:lo + 2 * LANES] = k_pe[h % 2]
    v_ref[...] = kv[:, n_nope:].astype(BF16)


def _mla_proj(x, pos, invf, w_down, qg, kvg, w_qb, w_kvb):
    T = TOK_TILE
    return pl.pallas_call(
        _mla_proj_kernel,
        grid=(N_TOK // T,),
        in_specs=[
            pl.BlockSpec((T, D_MODEL), lambda i: (i, 0)),
            pl.BlockSpec((T, 1), lambda i: (i, 0)),
            _resident((1, LANES)),
            _resident((D_MODEL, DOWN_WIDTH)),
            _resident((1, Q_LORA_RANK)),
            _resident((1, KV_LORA_RANK)),
            _resident((Q_LORA_RANK, QB_WIDTH)),
            _resident((KV_LORA_RANK, 2 * V_WIDTH)),
        ],
        out_specs=[
            pl.BlockSpec((T, QK_WIDTH), lambda i: (i, 0)),
            pl.BlockSpec((T, QK_WIDTH), lambda i: (i, 0)),
            pl.BlockSpec((T, V_WIDTH), lambda i: (i, 0)),
        ],
        out_shape=[
            jax.ShapeDtypeStruct((N_TOK, QK_WIDTH), BF16),
            jax.ShapeDtypeStruct((N_TOK, QK_WIDTH), BF16),
            jax.ShapeDtypeStruct((N_TOK, V_WIDTH), BF16),
        ],
        compiler_params=pltpu.CompilerParams(
            dimension_semantics=("arbitrary",),
            vmem_limit_bytes=VMEM_LIMIT),
        name="mla_proj",
    )(x, pos, invf, w_down, qg, kvg, w_qb, w_kvb)


_NT = (((1,), (1,)), ((), ()))


def _mla_attn_kernel(q_ref, k_ref, v_ref, o_ref, v_ones):
    c = (QK_NOPE_DIM + QK_ROPE_DIM) ** -0.5 * float(np.log2(np.e))
    neg = jnp.finfo(F32).min
    R = ATTN_Q_ROWS
    row = lax.broadcasted_iota(jnp.int32, (R, R), 0)
    col = lax.broadcasted_iota(jnp.int32, (R, R), 1)
    hidden = (col // CHUNK) > (row // CHUNK)
    for hh in range(ATTN_HEADS_PER_STEP):
        v_ones[hh, :, :V_HEAD_DIM] = v_ref[:, hh * V_HEAD_DIM:(hh + 1) * V_HEAD_DIM]
        v_ones[hh, :, V_HEAD_DIM:] = jnp.ones((SEQ, LANES), BF16)
    for j in range(SEQ // R):
        qs, ke = j * R, (j + 1) * R
        for hh in range(ATTN_HEADS_PER_STEP):
            lo, hi = hh * QK_HEAD_WIDTH, (hh + 1) * QK_HEAD_WIDTH
            s = lax.dot_general(q_ref[qs:ke, lo:hi], k_ref[:ke, lo:hi], _NT,
                                preferred_element_type=F32) * c
            diag = jnp.where(hidden, neg, s[:, qs:ke])
            s = diag if j == 0 else jnp.concatenate([s[:, :qs], diag], axis=-1)
            m = jnp.max(s, axis=-1, keepdims=True)
            p = jnp.exp2(s - m).astype(BF16)
            ol = jnp.dot(p, v_ones[hh, :ke, :], preferred_element_type=F32)
            o = ol[:, :V_HEAD_DIM] / ol[:, V_HEAD_DIM:V_HEAD_DIM + 1]
            o_ref[qs:ke, hh * V_HEAD_DIM:(hh + 1) * V_HEAD_DIM] = o.astype(BF16)


def _mla_attn(q, k, v):
    hps = ATTN_HEADS_PER_STEP
    blk = lambda b, g: (b, 0, g)
    return pl.pallas_call(
        _mla_attn_kernel,
        grid=(BATCH, MLA_HEADS // hps),
        in_specs=[
            pl.BlockSpec((None, SEQ, hps * QK_HEAD_WIDTH), blk),
            pl.BlockSpec((None, SEQ, hps * QK_HEAD_WIDTH), blk),
            pl.BlockSpec((None, SEQ, hps * V_HEAD_DIM), blk),
        ],
        out_specs=pl.BlockSpec((None, SEQ, hps * V_HEAD_DIM), blk),
        out_shape=jax.ShapeDtypeStruct((BATCH, SEQ, V_WIDTH), BF16),
        scratch_shapes=[pltpu.VMEM((hps, SEQ, V_HEAD_DIM + LANES), BF16)],
        compiler_params=pltpu.CompilerParams(
            dimension_semantics=("arbitrary", "arbitrary"),
            vmem_limit_bytes=VMEM_LIMIT),
        name="mla_attn",
    )(q, k, v)


def _q_weight(w_qb):
    w = w_qb.reshape(Q_LORA_RANK, MLA_HEADS, QK_NOPE_DIM + QK_ROPE_DIM)
    nope = w[:, :, :QK_NOPE_DIM].reshape(Q_LORA_RANK, MLA_HEADS * QK_NOPE_DIM)
    pe = w[:, :, QK_NOPE_DIM:].reshape(Q_LORA_RANK, MLA_HEADS // 2, 2, 2, ROPE_HALF)
    pe = pe.transpose(0, 1, 3, 2, 4).reshape(Q_LORA_RANK, (MLA_HEADS // 2) * LANES)
    return jnp.concatenate([nope, pe], axis=-1).astype(BF16)


def _kv_weight(w_kvb):
    w = w_kvb.reshape(KV_LORA_RANK, MLA_HEADS, 2, QK_NOPE_DIM)
    return w.transpose(0, 2, 1, 3).reshape(KV_LORA_RANK, 2 * V_WIDTH).astype(BF16)


def _down_weight(w_down):
    base = Q_LORA_RANK + KV_LORA_RANK
    x1 = w_down[:, base:base + ROPE_HALF]
    x2 = w_down[:, base + ROPE_HALF:]
    return jnp.concatenate([w_down[:, :base], x1, x1, x2, x2], axis=-1).astype(BF16)


def _pair_block_diag(w):
    w = w.reshape(LRU_HEADS // 2, 2, LRU_HEAD_DIM, LRU_HEAD_DIM)
    z = jnp.zeros_like(w[:, 0])
    top = jnp.concatenate([w[:, 0], z], axis=-1)
    bot = jnp.concatenate([z, w[:, 1]], axis=-1)
    return jnp.concatenate([top, bot], axis=-2)


def kernel(x, positions, ln_mix_g, ln_mix_b, ln_ffn_g, ln_ffn_b, even_w_in, pool_w, pool_scale, lru_conv_w, lru_conv_b, lru_w_a, lru_b_a, lru_w_x, lru_b_x, lru_lambda, even_w_out, mla_w_down, mla_q_norm_g, mla_kv_norm_g, mla_w_qb, mla_w_kvb, mla_w_o, mlp_w1, mlp_w2):
    row = lambda v: v.reshape(1, -1)
    inv_freq = ROPE_THETA ** (-jnp.arange(0, QK_ROPE_DIM, 2, dtype=F32) / QK_ROPE_DIM)
    invf = jnp.tile(inv_freq, LANES // ROPE_HALF).reshape(1, LANES)
    pos = positions.reshape(N_TOK, 1)

    xf = x.reshape(N_TOK, D_MODEL)
    for layer in range(DEPTH):
        j = layer // 2
        if layer % 2 == 0:
            y = _even_mix(
                xf.reshape(BATCH, SEQ, D_MODEL), even_w_in[j].astype(BF16), pool_w[j].astype(BF16),
                row(pool_scale[j]), lru_conv_w[j], row(lru_conv_b[j]),
                _pair_block_diag(0.5 * lru_w_a[j]).astype(BF16),
                _pair_block_diag(0.5 * lru_w_x[j]).astype(BF16),
                row(0.5 * lru_b_a[j]), row(0.5 * lru_b_x[j]), row(lru_lambda[j]))
            y = y.reshape(N_TOK, EVEN_MIX_WIDTH)
            w_out = even_w_out
        else:
            q, k, v = _mla_proj(
                xf, pos, invf, _down_weight(mla_w_down[j]),
                row(mla_q_norm_g[j]), row(mla_kv_norm_g[j]),
                _q_weight(mla_w_qb[j]), _kv_weight(mla_w_kvb[j]))
            y = _mla_attn(q.reshape(BATCH, SEQ, QK_WIDTH), k.reshape(BATCH, SEQ, QK_WIDTH),
                          v.reshape(BATCH, SEQ, V_WIDTH)).reshape(N_TOK, V_WIDTH)
            w_out = mla_w_o
        xf = _out_mlp(layer, y, xf, w_out, row(ln_mix_g[layer]), row(ln_mix_b[layer]),
                      mlp_w1, mlp_w2, row(ln_ffn_g[layer]), row(ln_ffn_b[layer]))
    return xf.reshape(BATCH, SEQ, D_MODEL)
```

```python
import functools

import numpy as np
import jax
import jax.numpy as jnp
from jax import lax
from jax.experimental import pallas as pl
from jax.experimental.pallas import tpu as pltpu

D_MODEL = 1024
BATCH = 8
SEQ = 2048
DEPTH = 4
N_TOK = BATCH * SEQ

CHUNK = 64

LANES = 128
SUBLANES = 8
assert BATCH == SUBLANES

POOL_WINDOWS = (2, 4, 8, 16)
POOL_GROUP_WIDTH = 128
POOL_WIDTH = 512
POOL_HIST = 16
LRU_WIDTH = 1024
LRU_HEADS = 8
LRU_HEAD_DIM = 128
CONV_WIDTH = 4
CONV_HIST = CONV_WIDTH - 1
LRU_C = 8.0
EVEN_IN_WIDTH = POOL_WIDTH + 2 * LRU_WIDTH
EVEN_MIX_WIDTH = POOL_WIDTH + LRU_WIDTH

MLA_HEADS = 8
QK_NOPE_DIM = 128
QK_ROPE_DIM = 64
ROPE_HALF = QK_ROPE_DIM // 2
V_HEAD_DIM = 128
Q_LORA_RANK = 384
KV_LORA_RANK = 256
ROPE_THETA = 10000.0
DOWN_WIDTH = Q_LORA_RANK + KV_LORA_RANK + LANES
QB_WIDTH = MLA_HEADS * QK_NOPE_DIM + (MLA_HEADS // 2) * LANES
QK_HEAD_WIDTH = 2 * LANES
QK_WIDTH = MLA_HEADS * QK_HEAD_WIDTH
V_WIDTH = MLA_HEADS * V_HEAD_DIM
ATTN_Q_ROWS = 512
ATTN_HEADS_PER_STEP = 4

D_FF = 4 * D_MODEL
FF_CHUNK = 512
DEEPNORM_ALPHA = (2 * DEPTH) ** 0.25
LN_EPS = 1e-5
RMS_EPS = 1e-6

MIX_FRAMES = 128
MIX_ROWS = MIX_FRAMES * BATCH
TOK_TILE = 1024
STAGE_ROWS = 512
VMEM_LIMIT = 60 * 1024 * 1024

F32 = jnp.float32
BF16 = jnp.bfloat16
F32_TINY = float(np.finfo(np.float32).tiny)


def _resident(shape):
    zeros = (0,) * len(shape)
    return pl.BlockSpec(shape, lambda *_: zeros, pipeline_mode=pl.Buffered(1))


def _layer_norm(v, g, b):
    mu = jnp.mean(v, axis=-1, keepdims=True)
    d = v - mu
    var = jnp.mean(d * d, axis=-1, keepdims=True)
    return d * lax.rsqrt(var + LN_EPS) * g + b


def _rms_norm(v, g):
    return v * lax.rsqrt(jnp.mean(v * v, axis=-1, keepdims=True) + RMS_EPS) * g


def _even_mix_kernel(x_ref, w_in_ref, pool_w_ref, pool_scale_ref, conv_w_ref, conv_b_ref,
                     wa_ref, wx_ref, ba_ref, bx_ref, lam_ref, y_ref,
                     pool_ext, lru_ext, carry, h_buf, x_tm, y_tm):
    t = pl.program_id(0)
    R = MIX_ROWS
    F = MIX_FRAMES
    PH = POOL_HIST * BATCH
    CH = CONV_HIST * BATCH

    @pl.when(t == 0)
    def _():
        pool_ext[0:PH, :] = jnp.zeros((PH, POOL_WIDTH), F32)
        lru_ext[0:CH, :] = jnp.zeros((CH, LRU_WIDTH), F32)
        carry[...] = jnp.zeros((BATCH, LRU_WIDTH), F32)

    for b in range(BATCH):
        for cs in range(D_MODEL // LANES):
            x_tm[cs, pl.ds(b, F, stride=BATCH), :] = x_ref[b, :, cs * LANES:(cs + 1) * LANES]
    xb = jnp.concatenate([x_tm[cs] for cs in range(D_MODEL // LANES)], axis=-1).astype(BF16)
    proj = jnp.dot(xb, w_in_ref[...], preferred_element_type=F32)
    u_pool = proj[:, :POOL_WIDTH]
    u_lru = proj[:, POOL_WIDTH:POOL_WIDTH + LRU_WIDTH]
    u_gate = proj[:, POOL_WIDTH + LRU_WIDTH:]
    pool_ext[PH:, :] = u_pool
    lru_ext[CH:, :] = u_lru

    frame = lax.broadcasted_iota(jnp.int32, (R, 1), 0) // BATCH + t * MIX_FRAMES
    for g, w in enumerate(POOL_WINDOWS):
        lo, hi = g * POOL_GROUP_WIDTH, (g + 1) * POOL_GROUP_WIDTH
        acc = pool_ext[:, lo:hi]
        k = 1
        while k < w:
            acc = acc[k * BATCH:, :] + acc[:acc.shape[0] - k * BATCH, :]
            k *= 2
        acc = acc[acc.shape[0] - R:, :]
        count = jnp.minimum(frame + 1, w).astype(F32)
        d = acc / count - u_pool[:, lo:hi]
        yg = jnp.dot(d.astype(BF16), pool_w_ref[g], preferred_element_type=F32)
        y_tm[g] = yg * pool_scale_ref[:, lo:hi]
    pool_ext[0:PH, :] = u_pool[R - PH:, :]

    cw = conv_w_ref[...]
    c = cw[CONV_WIDTH - 1:CONV_WIDTH, :] * u_lru + conv_b_ref[...]
    for k in range(CONV_WIDTH - 1):
        start = CH - (CONV_WIDTH - 1 - k) * BATCH
        c = c + cw[k:k + 1, :] * lru_ext[start:start + R, :]
    lru_ext[0:CH, :] = u_lru[R - CH:, :]

    cb = c.astype(BF16)
    za, zx = [], []
    for p in range(LRU_HEADS // 2):
        cp = cb[:, p * 2 * LRU_HEAD_DIM:(p + 1) * 2 * LRU_HEAD_DIM]
        za.append(jnp.dot(cp, wa_ref[p], preferred_element_type=F32))
        zx.append(jnp.dot(cp, wx_ref[p], preferred_element_type=F32))
    ta = jnp.tanh(jnp.concatenate(za, axis=-1) + ba_ref[...])
    tx = jnp.tanh(jnp.concatenate(zx, axis=-1) + bx_ref[...])

    z = -lam_ref[...]
    softplus = jnp.maximum(z, 0.0) + jnp.log1p(jnp.exp(-jnp.abs(z)))
    half_rate = (-0.5 * LRU_C) * softplus
    log_a = ta * half_rate + half_rate
    a = jnp.exp(log_a)
    m2 = jnp.tanh(-log_a) * (a * a + 1.0)
    mult = m2 * lax.rsqrt(jnp.maximum(m2, F32_TINY))
    half_c = 0.5 * c
    xin = mult * (tx * half_c + half_c)

    h = carry[...]
    for j in range(F):
        rows = slice(j * BATCH, (j + 1) * BATCH)
        h = a[rows, :] * h + xin[rows, :]
        h_buf[rows, :] = h
    carry[...] = h

    y_lru = h_buf[...] * jax.nn.gelu(u_gate)
    n_pool = len(POOL_WINDOWS)
    for cs in range(LRU_WIDTH // LANES):
        y_tm[n_pool + cs] = y_lru[:, cs * LANES:(cs + 1) * LANES]
    for b in range(BATCH):
        for cs in range(EVEN_MIX_WIDTH // LANES):
            y_ref[b, :, cs * LANES:(cs + 1) * LANES] = (
                y_tm[cs, pl.ds(b, F, stride=BATCH), :].astype(BF16))


def _even_mix(x, w_in, pool_w, pool_scale, conv_w, conv_b, wa, wx, ba, bx, lam):
    R = MIX_ROWS
    return pl.pallas_call(
        _even_mix_kernel,
        grid=(SEQ // MIX_FRAMES,),
        in_specs=[
            pl.BlockSpec((BATCH, MIX_FRAMES, D_MODEL), lambda t: (0, t, 0)),
            _resident((D_MODEL, EVEN_IN_WIDTH)),
            _resident((len(POOL_WINDOWS), POOL_GROUP_WIDTH, POOL_GROUP_WIDTH)),
            _resident((1, POOL_WIDTH)),
            _resident((CONV_WIDTH, LRU_WIDTH)),
            _resident((1, LRU_WIDTH)),
            _resident((LRU_HEADS // 2, 2 * LRU_HEAD_DIM, 2 * LRU_HEAD_DIM)),
            _resident((LRU_HEADS // 2, 2 * LRU_HEAD_DIM, 2 * LRU_HEAD_DIM)),
            _resident((1, LRU_WIDTH)),
            _resident((1, LRU_WIDTH)),
            _resident((1, LRU_WIDTH)),
        ],
        out_specs=pl.BlockSpec((BATCH, MIX_FRAMES, EVEN_MIX_WIDTH), lambda t: (0, t, 0)),
        out_shape=jax.ShapeDtypeStruct((BATCH, SEQ, EVEN_MIX_WIDTH), BF16),
        scratch_shapes=[
            pltpu.VMEM((R + POOL_HIST * BATCH, POOL_WIDTH), F32),
            pltpu.VMEM((R + CONV_HIST * BATCH, LRU_WIDTH), F32),
            pltpu.VMEM((BATCH, LRU_WIDTH), F32),
            pltpu.VMEM((R, LRU_WIDTH), F32),
            pltpu.VMEM((D_MODEL // LANES, R, LANES), F32),
            pltpu.VMEM((EVEN_MIX_WIDTH // LANES, R, LANES), F32),
        ],
        compiler_params=pltpu.CompilerParams(
            dimension_semantics=("arbitrary",),
            vmem_limit_bytes=VMEM_LIMIT),
        name="even_mix",
    )(x, w_in, pool_w, pool_scale, conv_w, conv_b, wa, wx, ba, bx, lam)


def _weight_chunks(src, dst):
    rows, cols = dst.shape
    return [(src.at[r:r + STAGE_ROWS, c:c + D_MODEL], dst.at[r:r + STAGE_ROWS, c:c + D_MODEL])
            for r in range(0, rows, STAGE_ROWS) for c in range(0, cols, D_MODEL)]


def _load_weights_as_bf16(chunks, stage, sem):
    copy = lambda i: pltpu.make_async_copy(chunks[i][0], stage.at[i % 2], sem.at[i % 2])
    copy(0).start()
    for i in range(len(chunks)):
        if i + 1 < len(chunks):
            copy(i + 1).start()
        copy(i).wait()
        chunks[i][1][...] = stage[i % 2].astype(BF16)


def _out_mlp_kernel(layer, y_ref, x_ref, w_out_hbm, g1_ref, b1_ref, w1_hbm, w2_hbm, g2_ref, b2_ref,
                    o_ref, w_out_ref, w1_ref, w2_ref, stage, sem):
    @pl.when(pl.program_id(0) == 0)
    def _():
        _load_weights_as_bf16(
            _weight_chunks(w_out_hbm.at[layer // 2], w_out_ref)
            + _weight_chunks(w1_hbm.at[layer], w1_ref)
            + _weight_chunks(w2_hbm.at[layer], w2_ref), stage, sem)

    mix = jnp.dot(y_ref[...], w_out_ref[...], preferred_element_type=F32)
    x1 = _layer_norm(DEEPNORM_ALPHA * x_ref[...] + mix, g1_ref[...], b1_ref[...])
    xb = x1.astype(BF16)
    acc = None
    for c in range(D_FF // FF_CHUNK):
        lo, hi = c * FF_CHUNK, (c + 1) * FF_CHUNK
        h = jnp.dot(xb, w1_ref[:, lo:hi], preferred_element_type=F32)
        h = jnp.square(jnp.maximum(h, 0.0)).astype(BF16)
        part = jnp.dot(h, w2_ref[lo:hi, :], preferred_element_type=F32)
        acc = part if acc is None else acc + part
    o_ref[...] = _layer_norm(DEEPNORM_ALPHA * x1 + acc, g2_ref[...], b2_ref[...])


def _out_mlp(layer, y, x, w_out_all, g1, b1, w1_all, w2_all, g2, b2):
    T = TOK_TILE
    kd = y.shape[-1]
    hbm = pl.BlockSpec(memory_space=pl.ANY)
    return pl.pallas_call(
        functools.partial(_out_mlp_kernel, layer),
        grid=(N_TOK // T,),
        in_specs=[
            pl.BlockSpec((T, kd), lambda i: (i, 0)),
            pl.BlockSpec((T, D_MODEL), lambda i: (i, 0)),
            hbm,
            _resident((1, D_MODEL)),
            _resident((1, D_MODEL)),
            hbm,
            hbm,
            _resident((1, D_MODEL)),
            _resident((1, D_MODEL)),
        ],
        out_specs=pl.BlockSpec((T, D_MODEL), lambda i: (i, 0)),
        out_shape=jax.ShapeDtypeStruct((N_TOK, D_MODEL), F32),
        scratch_shapes=[
            pltpu.VMEM((kd, D_MODEL), BF16),
            pltpu.VMEM((D_MODEL, D_FF), BF16),
            pltpu.VMEM((D_FF, D_MODEL), BF16),
            pltpu.VMEM((2, STAGE_ROWS, D_MODEL), F32),
            pltpu.SemaphoreType.DMA((2,)),
        ],
        compiler_params=pltpu.CompilerParams(
            dimension_semantics=("arbitrary",),
            vmem_limit_bytes=VMEM_LIMIT),
        name="out_mlp",
    )(y, x, w_out_all, g1, b1, w1_all, w2_all, g2, b2)


def _mla_proj_kernel(x_ref, pos_ref, invf_ref, w_down_ref, qg_ref, kvg_ref, w_qb_ref, w_kvb_ref,
                     q_ref, k_ref, v_ref):
    xb = x_ref[...].astype(BF16)
    down = jnp.dot(xb, w_down_ref[...], preferred_element_type=F32)
    cq = _rms_norm(down[:, :Q_LORA_RANK], qg_ref[...])
    ckv = _rms_norm(down[:, Q_LORA_RANK:Q_LORA_RANK + KV_LORA_RANK], kvg_ref[...])
    kpe = down[:, Q_LORA_RANK + KV_LORA_RANK:]

    ang = pos_ref[...].astype(F32) * invf_ref[...]
    cos = jnp.cos(ang)
    lane = lax.broadcasted_iota(jnp.int32, (1, LANES), 1)
    sin = jnp.sin(ang) * jnp.where(lane < 2 * ROPE_HALF, -1.0, 1.0)

    def rope(blk):
        return blk * cos + pltpu.roll(blk, 2 * ROPE_HALF, axis=1) * sin

    q = jnp.dot(cq.astype(BF16), w_qb_ref[...], preferred_element_type=F32)
    n_nope = MLA_HEADS * QK_NOPE_DIM
    q_pe = [rope(q[:, n_nope + p * LANES:n_nope + (p + 1) * LANES]).astype(BF16)
            for p in range(MLA_HEADS // 2)]
    kv = jnp.dot(ckv.astype(BF16), w_kvb_ref[...], preferred_element_type=F32)
    kr = rope(kpe)
    first_slot = (lane % (2 * ROPE_HALF)) < ROPE_HALF
    k_pe = [jnp.where(first_slot, kr, 0.0).astype(BF16),
            jnp.where(first_slot, 0.0, kr).astype(BF16)]
    for h in range(MLA_HEADS):
        lo = h * QK_HEAD_WIDTH
        q_ref[:, lo:lo + LANES] = q[:, h * LANES:(h + 1) * LANES].astype(BF16)
        q_ref[:, lo + LANES:lo + 2 * LANES] = q_pe[h // 2]
        k_ref[:, lo:lo + LANES] = kv[:, h * LANES:(h + 1) * LANES].astype(BF16)
        k_ref[:, lo + LANES:lo + 2 * LANES] = k_pe[h % 2]
    v_ref[...] = kv[:, n_nope:].astype(BF16)


def _mla_proj(x, pos, invf, w_down, qg, kvg, w_qb, w_kvb):
    T = TOK_TILE
    return pl.pallas_call(
        _mla_proj_kernel,
        grid=(N_TOK // T,),
        in_specs=[
            pl.BlockSpec((T, D_MODEL), lambda i: (i, 0)),
            pl.BlockSpec((T, 1), lambda i: (i, 0)),
            _resident((1, LANES)),
            _resident((D_MODEL, DOWN_WIDTH)),
            _resident((1, Q_LORA_RANK)),
            _resident((1, KV_LORA_RANK)),
            _resident((Q_LORA_RANK, QB_WIDTH)),
            _resident((KV_LORA_RANK, 2 * V_WIDTH)),
        ],
        out_specs=[
            pl.BlockSpec((T, QK_WIDTH), lambda i: (i, 0)),
            pl.BlockSpec((T, QK_WIDTH), lambda i: (i, 0)),
            pl.BlockSpec((T, V_WIDTH), lambda i: (i, 0)),
        ],
        out_shape=[
            jax.ShapeDtypeStruct((N_TOK, QK_WIDTH), BF16),
            jax.ShapeDtypeStruct((N_TOK, QK_WIDTH), BF16),
            jax.ShapeDtypeStruct((N_TOK, V_WIDTH), BF16),
        ],
        compiler_params=pltpu.CompilerParams(
            dimension_semantics=("arbitrary",),
            vmem_limit_bytes=VMEM_LIMIT),
        name="mla_proj",
    )(x, pos, invf, w_down, qg, kvg, w_qb, w_kvb)


_NT = (((1,), (1,)), ((), ()))


def _mla_attn_kernel(q_ref, k_ref, v_ref, o_ref, v_ones):
    c = (QK_NOPE_DIM + QK_ROPE_DIM) ** -0.5 * float(np.log2(np.e))
    neg = jnp.finfo(F32).min
    R = ATTN_Q_ROWS
    row = lax.broadcasted_iota(jnp.int32, (R, R), 0)
    col = lax.broadcasted_iota(jnp.int32, (R, R), 1)
    hidden = (col // CHUNK) > (row // CHUNK)
    for hh in range(ATTN_HEADS_PER_STEP):
        v_ones[hh, :, :V_HEAD_DIM] = v_ref[:, hh * V_HEAD_DIM:(hh + 1) * V_HEAD_DIM]
        v_ones[hh, :, V_HEAD_DIM:] = jnp.ones((SEQ, LANES), BF16)
    for j in range(SEQ // R):
        qs, ke = j * R, (j + 1) * R
        for hh in range(ATTN_HEADS_PER_STEP):
            lo, hi = hh * QK_HEAD_WIDTH, (hh + 1) * QK_HEAD_WIDTH
            s = lax.dot_general(q_ref[qs:ke, lo:hi], k_ref[:ke, lo:hi], _NT,
                                preferred_element_type=F32) * c
            diag = jnp.where(hidden, neg, s[:, qs:ke])
            s = diag if j == 0 else jnp.concatenate([s[:, :qs], diag], axis=-1)
            m = jnp.max(s, axis=-1, keepdims=True)
            p = jnp.exp2(s - m).astype(BF16)
            ol = jnp.dot(p, v_ones[hh, :ke, :], preferred_element_type=F32)
            o = ol[:, :V_HEAD_DIM] / ol[:, V_HEAD_DIM:V_HEAD_DIM + 1]
            o_ref[qs:ke, hh * V_HEAD_DIM:(hh + 1) * V_HEAD_DIM] = o.astype(BF16)


def _mla_attn(q, k, v):
    hps = ATTN_HEADS_PER_STEP
    blk = lambda b, g: (b, 0, g)
    return pl.pallas_call(
        _mla_attn_kernel,
        grid=(BATCH, MLA_HEADS // hps),
        in_specs=[
            pl.BlockSpec((None, SEQ, hps * QK_HEAD_WIDTH), blk),
            pl.BlockSpec((None, SEQ, hps * QK_HEAD_WIDTH), blk),
            pl.BlockSpec((None, SEQ, hps * V_HEAD_DIM), blk),
        ],
        out_specs=pl.BlockSpec((None, SEQ, hps * V_HEAD_DIM), blk),
        out_shape=jax.ShapeDtypeStruct((BATCH, SEQ, V_WIDTH), BF16),
        scratch_shapes=[pltpu.VMEM((hps, SEQ, V_HEAD_DIM + LANES), BF16)],
        compiler_params=pltpu.CompilerParams(
            dimension_semantics=("arbitrary", "arbitrary"),
            vmem_limit_bytes=VMEM_LIMIT),
        name="mla_attn",
    )(q, k, v)


def _q_weight(w_qb):
    w = w_qb.reshape(Q_LORA_RANK, MLA_HEADS, QK_NOPE_DIM + QK_ROPE_DIM)
    nope = w[:, :, :QK_NOPE_DIM].reshape(Q_LORA_RANK, MLA_HEADS * QK_NOPE_DIM)
    pe = w[:, :, QK_NOPE_DIM:].reshape(Q_LORA_RANK, MLA_HEADS // 2, 2, 2, ROPE_HALF)
    pe = pe.transpose(0, 1, 3, 2, 4).reshape(Q_LORA_RANK, (MLA_HEADS // 2) * LANES)
    return jnp.concatenate([nope, pe], axis=-1).astype(BF16)


def _kv_weight(w_kvb):
    w = w_kvb.reshape(KV_LORA_RANK, MLA_HEADS, 2, QK_NOPE_DIM)
    return w.transpose(0, 2, 1, 3).reshape(KV_LORA_RANK, 2 * V_WIDTH).astype(BF16)


def _down_weight(w_down):
    base = Q_LORA_RANK + KV_LORA_RANK
    x1 = w_down[:, base:base + ROPE_HALF]
    x2 = w_down[:, base + ROPE_HALF:]
    return jnp.concatenate([w_down[:, :base], x1, x1, x2, x2], axis=-1).astype(BF16)


def _pair_block_diag(w):
    w = w.reshape(LRU_HEADS // 2, 2, LRU_HEAD_DIM, LRU_HEAD_DIM)
    z = jnp.zeros_like(w[:, 0])
    top = jnp.concatenate([w[:, 0], z], axis=-1)
    bot = jnp.concatenate([z, w[:, 1]], axis=-1)
    return jnp.concatenate([top, bot], axis=-2)


def kernel(x, positions, ln_mix_g, ln_mix_b, ln_ffn_g, ln_ffn_b, even_w_in, pool_w, pool_scale, lru_conv_w, lru_conv_b, lru_w_a, lru_b_a, lru_w_x, lru_b_x, lru_lambda, even_w_out, mla_w_down, mla_q_norm_g, mla_kv_norm_g, mla_w_qb, mla_w_kvb, mla_w_o, mlp_w1, mlp_w2):
    row = lambda v: v.reshape(1, -1)
    inv_freq = ROPE_THETA ** (-jnp.arange(0, QK_ROPE_DIM, 2, dtype=F32) / QK_ROPE_DIM)
    invf = jnp.tile(inv_freq, LANES // ROPE_HALF).reshape(1, LANES)
    pos = positions.reshape(N_TOK, 1)

    xf = x.reshape(N_TOK, D_MODEL)
    for layer in range(DEPTH):
        j = layer // 2
        if layer % 2 == 0:
            y = _even_mix(
                xf.reshape(BATCH, SEQ, D_MODEL), even_w_in[j].astype(BF16), pool_w[j].astype(BF16),
                row(pool_scale[j]), lru_conv_w[j], row(lru_conv_b[j]),
                _pair_block_diag(0.5 * lru_w_a[j]).astype(BF16),
                _pair_block_diag(0.5 * lru_w_x[j]).astype(BF16),
                row(0.5 * lru_b_a[j]), row(0.5 * lru_b_x[j]), row(lru_lambda[j]))
            y = y.reshape(N_TOK, EVEN_MIX_WIDTH)
            w_out = even_w_out
        else:
            q, k, v = _mla_proj(
                xf, pos, invf, _down_weight(mla_w_down[j]),
                row(mla_q_norm_g[j]), row(mla_kv_norm_g[j]),
                _q_weight(mla_w_qb[j]), _kv_weight(mla_w_kvb[j]))
            y = _mla_attn(q.reshape(BATCH, SEQ, QK_WIDTH), k.reshape(BATCH, SEQ, QK_WIDTH),
                          v.reshape(BATCH, SEQ, V_WIDTH)).reshape(N_TOK, V_WIDTH)
            w_out = mla_w_o
        xf = _out_mlp(layer, y, xf, w_out, row(ln_mix_g[layer]), row(ln_mix_b[layer]),
                      mlp_w1, mlp_w2, row(ln_ffn_g[layer]), row(ln_ffn_b[layer]))
    return xf.reshape(BATCH, SEQ, D_MODEL)
```
